```python
import math
import jax, jax.numpy as jnp
from jax import lax
import numpy as np

D_MODEL = 1024
BATCH = 4
SEQ = 4096
DEPTH = 1

MEM_LEN = 256
D_FF = 2752
MLA_HEADS = 4
MLA_Q_RANK = 384
MLA_KV_RANK = 256
MLA_NOPE = 128
MLA_ROPE = 64
MLA_V = 128
MLA_QK = MLA_NOPE + MLA_ROPE
MLA_WIDTH = MLA_HEADS * MLA_V
SSM_WIDTH = D_MODEL - MLA_WIDTH
SSM_GROUP = 16
SSM_GROUPS = SSM_WIDTH // SSM_GROUP
SSM_STATE = 64
DT_MIN = 1e-3
DT_MAX = 1e-1
XATTN_HEADS = 4
XATTN_HEAD_DIM = 128
XATTN_WIDTH = XATTN_HEADS * XATTN_HEAD_DIM
IN_SPLITS = [MLA_Q_RANK, MLA_Q_RANK + MLA_KV_RANK, MLA_Q_RANK + MLA_KV_RANK + MLA_ROPE]
IN_WIDTH = MLA_Q_RANK + MLA_KV_RANK + MLA_ROPE + SSM_WIDTH
Q_BLOCK = 128
ROPE_THETA = 10000.0
EPS = 1e-6

kernel_name = "hymba_mla_s5_macaron_memxattn"


def rms_norm(x, g):
    xf = x.astype(jnp.float32)
    y = xf * lax.rsqrt(jnp.mean(xf * xf, axis=-1, keepdims=True) + EPS)
    return (y * g.astype(jnp.float32)).astype(x.dtype)


def swiglu(h, w_gate, w_up, w_down):
    return (jax.nn.silu(h @ w_gate) * (h @ w_up)) @ w_down


def rope(x, pos):
    half = x.shape[-1] // 2
    inv = ROPE_THETA ** (-jnp.arange(half, dtype=jnp.float32) / half)
    ang = (pos.astype(jnp.float32)[..., None] * inv)[:, :, None, :]
    cos, sin = jnp.cos(ang), jnp.sin(ang)
    x1 = x[..., :half].astype(jnp.float32)
    x2 = x[..., half:].astype(jnp.float32)
    return jnp.concatenate([x1 * cos - x2 * sin, x2 * cos + x1 * sin], axis=-1).astype(x.dtype)


def causal_block_attention(q, k, v):
    B, S, H, Dk = q.shape
    Dv = v.shape[-1]
    nb = S // Q_BLOCK
    scale = Dk ** -0.5
    qb = q.reshape(B, nb, Q_BLOCK, H, Dk).transpose(1, 0, 2, 3, 4)
    kpos = jnp.arange(S)

    def one_block(args):
        q_blk, i = args
        s = jnp.einsum('bqhd,bkhd->bhqk', q_blk, k).astype(jnp.float32) * scale
        qpos = i * Q_BLOCK + jnp.arange(Q_BLOCK)
        s = jnp.where(qpos[:, None] >= kpos[None, :], s, -jnp.inf)
        p = jax.nn.softmax(s, axis=-1).astype(v.dtype)
        return jnp.einsum('bhqk,bkhd->bqhd', p, v)

    out = lax.map(one_block, (qb, jnp.arange(nb)))
    return out.transpose(1, 0, 2, 3, 4).reshape(B, S, H, Dv)


def mla_mixer(c_q_in, c_kv_in, k_r, pos, q_norm, w_uq, kv_norm, w_ukv, qk_norm_q, qk_norm_k):
    B, S, _ = c_q_in.shape
    c_q = rms_norm(c_q_in, q_norm)
    q = (c_q @ w_uq).reshape(B, S, MLA_HEADS, MLA_QK)
    c_kv = rms_norm(c_kv_in, kv_norm)
    kv = (c_kv @ w_ukv).reshape(B, S, MLA_HEADS, MLA_NOPE + MLA_V)
    k_nope, v = kv[..., :MLA_NOPE], kv[..., MLA_NOPE:]
    k_rope = jnp.broadcast_to(k_r[:, :, None, :], (B, S, MLA_HEADS, MLA_ROPE))
    k = jnp.concatenate([k_nope, k_rope], axis=-1)
    q = rms_norm(q, qk_norm_q)
    k = rms_norm(k, qk_norm_k)
    q = jnp.concatenate([q[..., :MLA_NOPE], rope(q[..., MLA_NOPE:], pos)], axis=-1)
    k = jnp.concatenate([k[..., :MLA_NOPE], rope(k[..., MLA_NOPE:], pos)], axis=-1)
    o = causal_block_attention(q, k, v)
    return o.reshape(B, S, MLA_WIDTH)


def _complex_linear_combine(e1, e2):
    a1r, a1i, b1r, b1i = e1
    a2r, a2i, b2r, b2i = e2
    ar = a2r * a1r - a2i * a1i
    ai = a2r * a1i + a2i * a1r
    br = a2r * b1r - a2i * b1i + b2r
    bi = a2r * b1i + a2i * b1r + b2i
    return (ar, ai, br, bi)


def s5_mixer(u, a_re, a_im, log_dt, b_re, b_im, c_re, c_im, d, w_glu, b_glu):
    B, S, _ = u.shape
    f32 = jnp.float32
    uf = u.astype(f32).reshape(B, S, SSM_GROUPS, SSM_GROUP)
    lr, li = a_re.astype(f32), a_im.astype(f32)
    dt = jnp.exp(log_dt.astype(f32))[:, None]
    decay = jnp.exp(lr * dt)
    ar = decay * jnp.cos(li * dt)
    ai = decay * jnp.sin(li * dt)
    den = lr * lr + li * li
    nr = ar - 1.0
    coef_r = (nr * lr + ai * li) / den
    coef_i = (ai * lr - nr * li) / den
    br, bi = b_re.astype(f32), b_im.astype(f32)
    bbar_r = coef_r[..., None] * br - coef_i[..., None] * bi
    bbar_i = coef_r[..., None] * bi + coef_i[..., None] * br
    bu_r = jnp.einsum('bsgh,gph->bsgp', uf, bbar_r)
    bu_i = jnp.einsum('bsgh,gph->bsgp', uf, bbar_i)
    ar_t = jnp.broadcast_to(ar, bu_r.shape)
    ai_t = jnp.broadcast_to(ai, bu_r.shape)
    _, _, xr, xi = lax.associative_scan(_complex_linear_combine, (ar_t, ai_t, bu_r, bu_i), axis=1)
    y = (jnp.einsum('bsgp,ghp->bsgh', xr, c_re.astype(f32))
         - jnp.einsum('bsgp,ghp->bsgh', xi, c_im.astype(f32))
         + d.astype(f32) * uf)
    y = y.reshape(B, S, SSM_WIDTH)
    g = jax.nn.gelu(y)
    out = g * jax.nn.sigmoid(g @ w_glu.astype(f32) + b_glu.astype(f32))
    return out.astype(u.dtype)


def memory_cross_attention(h, mem, mem_norm, w_q, w_kv, qn, kn, w_o):
    B, S, _ = h.shape
    M = mem.shape[1]
    q = (h @ w_q).reshape(B, S, XATTN_HEADS, XATTN_HEAD_DIM)
    m = rms_norm(mem, mem_norm)
    kv = (m @ w_kv).reshape(B, M, 2, XATTN_HEADS, XATTN_HEAD_DIM)
    k, v = kv[:, :, 0], kv[:, :, 1]
    q = rms_norm(q, qn)
    k = rms_norm(k, kn)
    s = jnp.einsum('bshd,bmhd->bhsm', q, k).astype(jnp.float32) * (XATTN_HEAD_DIM ** -0.5)
    p = jax.nn.softmax(s, axis=-1).astype(v.dtype)
    o = jnp.einsum('bhsm,bmhd->bshd', p, v).reshape(B, S, XATTN_WIDTH)
    return o @ w_o


def setup_inputs(seed: int = 0) -> dict:
    key = jax.random.key(seed)
    ks = iter(jax.random.split(key, 48))
    f32 = jnp.float32

    def nrm(shape):
        return jax.random.normal(next(ks), (DEPTH,) + shape, f32)

    def w(shape, fan_in):
        return nrm(shape) * (fan_in ** -0.5)

    def gain(dim):
        return 1.0 + 0.02 * nrm((dim,))

    x = jax.random.normal(next(ks), (BATCH, SEQ, D_MODEL), f32)
    mem = jax.random.normal(next(ks), (BATCH, MEM_LEN, D_MODEL), f32)
    positions = (jax.random.randint(next(ks), (BATCH, 1), 0, 1024, jnp.int32)
                 + jnp.arange(SEQ, dtype=jnp.int32)[None, :])
    ffn1_norm = gain(D_MODEL)
    ffn1_w_gate = w((D_MODEL, D_FF), D_MODEL)
    ffn1_w_up = w((D_MODEL, D_FF), D_MODEL)
    ffn1_w_down = w((D_FF, D_MODEL), D_FF)
    mix_norm = gain(D_MODEL)
    w_in = w((D_MODEL, IN_WIDTH), D_MODEL)
    mla_q_norm = gain(MLA_Q_RANK)
    mla_w_uq = w((MLA_Q_RANK, MLA_HEADS * MLA_QK), MLA_Q_RANK)
    mla_kv_norm = gain(MLA_KV_RANK)
    mla_w_ukv = w((MLA_KV_RANK, MLA_HEADS * (MLA_NOPE + MLA_V)), MLA_KV_RANK)
    mla_qk_norm_q = gain(MLA_QK)
    mla_qk_norm_k = gain(MLA_QK)
    n = jnp.arange(SSM_STATE, dtype=f32)
    ssm_a_re = -0.5 + 0.01 * nrm((SSM_GROUPS, SSM_STATE))
    ssm_a_im = math.pi * n + 0.01 * nrm((SSM_GROUPS, SSM_STATE))
    ssm_log_dt = jax.random.uniform(next(ks), (DEPTH, SSM_GROUPS), f32, math.log(DT_MIN), math.log(DT_MAX))
    ssm_b_re = nrm((SSM_GROUPS, SSM_STATE, SSM_GROUP)) * (0.5 / SSM_GROUP) ** 0.5
    ssm_b_im = nrm((SSM_GROUPS, SSM_STATE, SSM_GROUP)) * (0.5 / SSM_GROUP) ** 0.5
    ssm_c_re = nrm((SSM_GROUPS, SSM_GROUP, SSM_STATE)) * (0.5 / SSM_STATE) ** 0.5
    ssm_c_im = nrm((SSM_GROUPS, SSM_GROUP, SSM_STATE)) * (0.5 / SSM_STATE) ** 0.5
    ssm_d = nrm((SSM_GROUPS, SSM_GROUP))
    ssm_w_glu = w((SSM_WIDTH, SSM_WIDTH), SSM_WIDTH)
    ssm_b_glu = 0.01 * nrm((SSM_WIDTH,))
    out_norm_mla = gain(MLA_WIDTH)
    out_norm_ssm = gain(SSM_WIDTH)
    w_o = w((MLA_WIDTH + SSM_WIDTH, D_MODEL), MLA_WIDTH + SSM_WIDTH)
    xattn_norm = gain(D_MODEL)
    mem_norm = gain(D_MODEL)
    xattn_w_q = w((D_MODEL, XATTN_WIDTH), D_MODEL)
    xattn_w_kv = w((D_MODEL, 2 * XATTN_WIDTH), D_MODEL)
    xattn_q_norm = gain(XATTN_HEAD_DIM)
    xattn_k_norm = gain(XATTN_HEAD_DIM)
    xattn_w_o = w((XATTN_WIDTH, D_MODEL), XATTN_WIDTH)
    ffn2_norm = gain(D_MODEL)
    ffn2_w_gate = w((D_MODEL, D_FF), D_MODEL)
    ffn2_w_up = w((D_MODEL, D_FF), D_MODEL)
    ffn2_w_down = w((D_FF, D_MODEL), D_FF)
    return {
        'x': x, 'mem': mem, 'positions': positions,
        'ffn1_norm': ffn1_norm, 'ffn1_w_gate': ffn1_w_gate, 'ffn1_w_up': ffn1_w_up, 'ffn1_w_down': ffn1_w_down,
        'mix_norm': mix_norm, 'w_in': w_in,
        'mla_q_norm': mla_q_norm, 'mla_w_uq': mla_w_uq, 'mla_kv_norm': mla_kv_norm, 'mla_w_ukv': mla_w_ukv,
        'mla_qk_norm_q': mla_qk_norm_q, 'mla_qk_norm_k': mla_qk_norm_k,
        'ssm_a_re': ssm_a_re, 'ssm_a_im': ssm_a_im, 'ssm_log_dt': ssm_log_dt,
        'ssm_b_re': ssm_b_re, 'ssm_b_im': ssm_b_im, 'ssm_c_re': ssm_c_re, 'ssm_c_im': ssm_c_im,
        'ssm_d': ssm_d, 'ssm_w_glu': ssm_w_glu, 'ssm_b_glu': ssm_b_glu,
        'out_norm_mla': out_norm_mla, 'out_norm_ssm': out_norm_ssm, 'w_o': w_o,
        'xattn_norm': xattn_norm, 'mem_norm': mem_norm, 'xattn_w_q': xattn_w_q, 'xattn_w_kv': xattn_w_kv,
        'xattn_q_norm': xattn_q_norm, 'xattn_k_norm': xattn_k_norm, 'xattn_w_o': xattn_w_o,
        'ffn2_norm': ffn2_norm, 'ffn2_w_gate': ffn2_w_gate, 'ffn2_w_up': ffn2_w_up, 'ffn2_w_down': ffn2_w_down,
    }


def reference(x, mem, positions,
              ffn1_norm, ffn1_w_gate, ffn1_w_up, ffn1_w_down,
              mix_norm, w_in,
              mla_q_norm, mla_w_uq, mla_kv_norm, mla_w_ukv, mla_qk_norm_q, mla_qk_norm_k,
              ssm_a_re, ssm_a_im, ssm_log_dt, ssm_b_re, ssm_b_im, ssm_c_re, ssm_c_im,
              ssm_d, ssm_w_glu, ssm_b_glu,
              out_norm_mla, out_norm_ssm, w_o,
              xattn_norm, mem_norm, xattn_w_q, xattn_w_kv, xattn_q_norm, xattn_k_norm, xattn_w_o,
              ffn2_norm, ffn2_w_gate, ffn2_w_up, ffn2_w_down):
    for l in range(DEPTH):
        x = x + 0.5 * swiglu(rms_norm(x, ffn1_norm[l]), ffn1_w_gate[l], ffn1_w_up[l], ffn1_w_down[l])
        h = rms_norm(x, mix_norm[l])
        proj = h @ w_in[l]
        c_q_in, c_kv_in, k_r, u = jnp.split(proj, IN_SPLITS, axis=-1)
        y_mla = mla_mixer(c_q_in, c_kv_in, k_r, positions, mla_q_norm[l], mla_w_uq[l],
                          mla_kv_norm[l], mla_w_ukv[l], mla_qk_norm_q[l], mla_qk_norm_k[l])
        y_ssm = s5_mixer(u, ssm_a_re[l], ssm_a_im[l], ssm_log_dt[l], ssm_b_re[l], ssm_b_im[l],
                         ssm_c_re[l], ssm_c_im[l], ssm_d[l], ssm_w_glu[l], ssm_b_glu[l])
        y = jnp.concatenate([rms_norm(y_mla, out_norm_mla[l]), rms_norm(y_ssm, out_norm_ssm[l])], axis=-1)
        x = x + y @ w_o[l]
        x = x + memory_cross_attention(rms_norm(x, xattn_norm[l]), mem, mem_norm[l], xattn_w_q[l],
                                       xattn_w_kv[l], xattn_q_norm[l], xattn_k_norm[l], xattn_w_o[l])
        x = x + 0.5 * swiglu(rms_norm(x, ffn2_norm[l]), ffn2_w_gate[l], ffn2_w_up[l], ffn2_w_down[l])
    return x
```

```python
import functools
import math

import jax
import jax.numpy as jnp
from jax import lax
from jax.experimental import pallas as pl
from jax.experimental.pallas import tpu as pltpu

F32 = jnp.float32
BF16 = jnp.bfloat16

D_MODEL = 1024
D_FF = 2752
MLA_HEADS = 4
MLA_Q_RANK = 384
MLA_KV_RANK = 256
MLA_NOPE = 128
MLA_ROPE = 64
MLA_V = 128
MLA_QK = MLA_NOPE + MLA_ROPE
MLA_WIDTH = MLA_HEADS * MLA_V
SSM_WIDTH = D_MODEL - MLA_WIDTH
SSM_GROUP = 16
SSM_GROUPS = SSM_WIDTH // SSM_GROUP
SSM_STATE = 64
XATTN_HEADS = 4
XATTN_HEAD_DIM = 128
XATTN_WIDTH = XATTN_HEADS * XATTN_HEAD_DIM
ROPE_THETA = 10000.0
EPS = 1e-6

LANES = 128
SUBLANES = 8
D_FF_PAD = 2816
FF_CHUNK = D_FF_PAD // 2
HEAD_PAD = 2 * LANES
ROPE_PAD = LANES
IN_PAD = MLA_Q_RANK + MLA_KV_RANK + ROPE_PAD + SSM_WIDTH
TOKEN_TILE = 512
ATTN_TILE = 512
SSM_STATES = SSM_GROUPS * SSM_STATE
SSM_TILE = 256
SSM_SUB = SSM_TILE // SUBLANES
TABLE_ROWS = 48
ROW_Q0 = SSM_SUB
ROW_Q8 = ROW_Q0 + SUBLANES
ROW_A64 = ROW_Q8 + 1
ROW_A128 = ROW_Q8 + 2
VMEM_LIMIT = 56 * 1024 * 1024


def _rms(x, g):
    return x * lax.rsqrt(jnp.mean(x * x, axis=-1, keepdims=True) + EPS) * g


def _dot(a, b):
    return jnp.dot(a, b, preferred_element_type=F32)


def _ffn_half_step(x, g_ref, wg_ref, wu_ref, wd_ref):
    h = _rms(x, g_ref[...]).astype(BF16)
    acc = None
    for c in range(D_FF_PAD // FF_CHUNK):
        lo = c * FF_CHUNK
        gate = _dot(h, wg_ref[:, lo:lo + FF_CHUNK])
        up = _dot(h, wu_ref[:, lo:lo + FF_CHUNK])
        act = (gate * jax.nn.sigmoid(gate) * up).astype(BF16)
        part = _dot(act, wd_ref[lo:lo + FF_CHUNK, :])
        acc = part if acc is None else acc + part
    return x + 0.5 * acc


def _rope(x, cos, sin_signed):
    return x * cos + pltpu.roll(x, ROPE_PAD // 2, 1) * sin_signed


def _front_kernel(x_ref, pos_ref, g1_ref, wg_ref, wu_ref, wd_ref, gmix_ref, win_ref,
                  gq_ref, wuq_ref, gkv_ref, wukv_ref, gqk_q_ref, gqk_k_ref, inv_ref, sgn_ref,
                  x1_ref, q_ref, k_ref, v_ref, u_ref):
    x1 = _ffn_half_step(x_ref[0], g1_ref, wg_ref, wu_ref, wd_ref)
    x1_ref[0] = x1
    h = _rms(x1, gmix_ref[...]).astype(BF16)
    proj = _dot(h, win_ref[...])
    o_ckv = MLA_Q_RANK
    o_kr = o_ckv + MLA_KV_RANK
    o_u = o_kr + ROPE_PAD
    u_ref[0] = proj[:, o_u:]
    cq = _rms(proj[:, :o_ckv], gq_ref[...]).astype(BF16)
    q_raw = _dot(cq, wuq_ref[...])
    ckv = _rms(proj[:, o_ckv:o_kr], gkv_ref[...]).astype(BF16)
    kv = _dot(ckv, wukv_ref[...])
    kr = proj[:, o_kr:o_u]
    ang = pos_ref[0].astype(F32) * inv_ref[...]
    cos = jnp.cos(ang)
    sin = jnp.sin(ang) * sgn_ref[...]
    kr_ss = jnp.sum(kr * kr, axis=-1, keepdims=True)
    gq = gqk_q_ref[...]
    gk = gqk_k_ref[...]
    inv_qk = 1.0 / MLA_QK
    scale = MLA_QK ** -0.5
    for hh in range(MLA_HEADS):
        qh = q_raw[:, hh * HEAD_PAD:(hh + 1) * HEAD_PAD]
        qn = qh * lax.rsqrt(jnp.sum(qh * qh, axis=-1, keepdims=True) * inv_qk + EPS) * gq
        q_rot = _rope(qn[:, MLA_NOPE:], cos, sin)
        q_ref[0, hh] = (jnp.concatenate([qn[:, :MLA_NOPE], q_rot], axis=-1) * scale).astype(BF16)
        kn = kv[:, hh * HEAD_PAD:hh * HEAD_PAD + MLA_NOPE]
        vv = kv[:, hh * HEAD_PAD + MLA_NOPE:(hh + 1) * HEAD_PAD]
        inv_k = lax.rsqrt((jnp.sum(kn * kn, axis=-1, keepdims=True) + kr_ss) * inv_qk + EPS)
        k_rot = _rope(kr * inv_k * gk[:, MLA_NOPE:], cos, sin)
        k_ref[0, hh] = jnp.concatenate([kn * inv_k * gk[:, :MLA_NOPE], k_rot], axis=-1).astype(BF16)
        v_ref[0, hh] = vv.astype(BF16)


def _const_spec(shape):
    return pl.BlockSpec(shape, lambda *_: (0,) * len(shape), pipeline_mode=pl.Buffered(1))


def _front(x, pos3, g1, wg, wu, wd, gmix, win, gq, wuq, gkv, wukv, gqk_q, gqk_k, inv_tab, sgn_tab):
    B, S, D = x.shape
    T = TOKEN_TILE
    row = lambda w: pl.BlockSpec((1, T, w), lambda b, i: (b, i, 0))
    head = lambda w: pl.BlockSpec((1, MLA_HEADS, T, w), lambda b, i: (b, 0, i, 0))
    consts = [g1, wg, wu, wd, gmix, win, gq, wuq, gkv, wukv, gqk_q, gqk_k, inv_tab, sgn_tab]
    return pl.pallas_call(
        _front_kernel,
        grid=(B, S // T),
        in_specs=[row(D), row(1)] + [_const_spec(c.shape) for c in consts],
        out_specs=[row(D), head(HEAD_PAD), head(HEAD_PAD), head(MLA_V), row(SSM_WIDTH)],
        out_shape=[
            jax.ShapeDtypeStruct((B, S, D), F32),
            jax.ShapeDtypeStruct((B, MLA_HEADS, S, HEAD_PAD), BF16),
            jax.ShapeDtypeStruct((B, MLA_HEADS, S, HEAD_PAD), BF16),
            jax.ShapeDtypeStruct((B, MLA_HEADS, S, MLA_V), BF16),
            jax.ShapeDtypeStruct((B, S, SSM_WIDTH), F32),
        ],
        compiler_params=pltpu.CompilerParams(
            dimension_semantics=("parallel", "parallel"), vmem_limit_bytes=VMEM_LIMIT),
        name="front",
    )(x, pos3, *consts)


def _attn_kernel(q_ref, k_ref, v_ref, o_ref):
    T = ATTN_TILE
    qi = pl.program_id(2)
    q = q_ref[0, 0]

    def scores(j):
        ks = k_ref[0, 0, pl.ds(pl.multiple_of(j * T, T), T), :]
        return lax.dot_general(q, ks, (((1,), (1,)), ((), ())), preferred_element_type=F32)

    def update(j, s, carry):
        m, l, acc = carry
        m_new = jnp.maximum(m, jnp.max(s, axis=-1, keepdims=True))
        alpha = jnp.exp(m - m_new)
        p = jnp.exp(s - m_new)
        l = alpha * l + jnp.sum(p, axis=-1, keepdims=True)
        vs = v_ref[0, 0, pl.ds(pl.multiple_of(j * T, T), T), :]
        acc = alpha * acc + _dot(p.astype(BF16), vs)
        return m_new, l, acc

    init = (jnp.full((T, 1), -jnp.inf, F32), jnp.zeros((T, 1), F32), jnp.zeros((T, MLA_V), F32))
    carry = lax.fori_loop(0, qi, lambda j, c: update(j, scores(j), c), init)
    rows = lax.broadcasted_iota(jnp.int32, (T, T), 0)
    cols = lax.broadcasted_iota(jnp.int32, (T, T), 1)
    s_diag = jnp.where(rows >= cols, scores(qi), -jnp.inf)
    _, l, acc = update(qi, s_diag, carry)
    o_ref[0] = acc / l


def _attention(q, k, v):
    B, H, S, _ = q.shape
    T = ATTN_TILE
    return pl.pallas_call(
        _attn_kernel,
        grid=(B, H, S // T),
        in_specs=[
            pl.BlockSpec((1, 1, T, HEAD_PAD), lambda b, h, i: (b, h, i, 0)),
            pl.BlockSpec((1, 1, S, HEAD_PAD), lambda b, h, i: (b, h, 0, 0)),
            pl.BlockSpec((1, 1, S, MLA_V), lambda b, h, i: (b, h, 0, 0)),
        ],
        out_specs=pl.BlockSpec((1, T, MLA_V), lambda b, h, i: (b, i, h)),
        out_shape=jax.ShapeDtypeStruct((B, S, H * MLA_V), F32),
        compiler_params=pltpu.CompilerParams(
            dimension_semantics=("parallel", "parallel", "parallel"), vmem_limit_bytes=VMEM_LIMIT),
        name="attn",
    )(q, k, v)


def _cmul(ar, ai, br, bi):
    return ar * br - ai * bi, ar * bi + ai * br


def _s5prep_kernel(lr_ref, li_ref, ldt_ref, btr_ref, bti_ref, tr_ref, ti_ref, bbr_ref, bbi_ref):
    lr = lr_ref[...]
    li = li_ref[...]
    dt = jnp.exp(ldt_ref[...])
    decay = jnp.exp(lr * dt)
    ar = decay * jnp.cos(li * dt)
    ai = decay * jnp.sin(li * dt)
    den = lr * lr + li * li
    nr = ar - 1.0
    coef_r = (nr * lr + ai * li) / den
    coef_i = (ai * lr - nr * li) / den
    br = btr_ref[...]
    bi = bti_ref[...]
    bbr_ref[...] = coef_r * br - coef_i * bi
    bbi_ref[...] = coef_r * bi + coef_i * br
    pr, pi = ar, ai
    tr_ref[0:1, :] = pr
    ti_ref[0:1, :] = pi
    for k in range(1, SSM_SUB):
        pr, pi = _cmul(pr, pi, ar, ai)
        tr_ref[k:k + 1, :] = pr
        ti_ref[k:k + 1, :] = pi
    a32r, a32i = pr, pi
    qr, qi = jnp.ones_like(ar), jnp.zeros_like(ar)
    for r in range(SUBLANES + 1):
        tr_ref[ROW_Q0 + r:ROW_Q0 + r + 1, :] = qr
        ti_ref[ROW_Q0 + r:ROW_Q0 + r + 1, :] = qi
        qr, qi = _cmul(qr, qi, a32r, a32i)
    a64r, a64i = _cmul(a32r, a32i, a32r, a32i)
    a128r, a128i = _cmul(a64r, a64i, a64r, a64i)
    tr_ref[ROW_A64:ROW_A64 + 1, :] = a64r
    ti_ref[ROW_A64:ROW_A64 + 1, :] = a64i
    tr_ref[ROW_A128:ROW_A128 + 1, :] = a128r
    ti_ref[ROW_A128:ROW_A128 + 1, :] = a128i
    zeros = jnp.zeros((TABLE_ROWS - ROW_A128 - 1, SSM_STATES), F32)
    tr_ref[ROW_A128 + 1:, :] = zeros
    ti_ref[ROW_A128 + 1:, :] = zeros


def _s5prep(lr, li, ldt, btr, bti):
    tab = jax.ShapeDtypeStruct((TABLE_ROWS, SSM_STATES), F32)
    bb = jax.ShapeDtypeStruct((SSM_GROUP, SSM_STATES), F32)
    return pl.pallas_call(_s5prep_kernel, out_shape=[tab, tab, bb, bb], name="s5prep")(lr, li, ldt, btr, bti)


def _s5_kernel(u_ref, perm_ref, permt_ref, bbig_ref, cbig_ref, tr_ref, ti_ref, d_ref, wglu_ref, bglu_ref,
               gout_ref, y_ref, st_ref, hr_ref, hi_ref):
    N = SSM_STATES

    @pl.when(pl.program_id(1) == 0)
    def _():
        hr_ref[...] = jnp.zeros_like(hr_ref)
        hi_ref[...] = jnp.zeros_like(hi_ref)

    u = u_ref[0]
    u_perm = _dot(perm_ref[...], u.astype(BF16)).astype(BF16)
    st_ref[...] = _dot(u_perm, bbig_ref[...])

    ar = tr_ref[0:1, :]
    ai = ti_ref[0:1, :]

    def local_step(k, carry):
        xr, xi = carry
        rows = pl.ds(pl.multiple_of(k * SUBLANES, SUBLANES), SUBLANES)
        nr = ar * xr - ai * xi + st_ref[rows, :N]
        ni = ar * xi + ai * xr + st_ref[rows, N:]
        st_ref[rows, :N] = nr
        st_ref[rows, N:] = ni
        return nr, ni

    zero = jnp.zeros((SUBLANES, N), F32)
    fr, fi = lax.fori_loop(0, SSM_SUB, local_step, (zero, zero))

    ridx = lax.broadcasted_iota(jnp.int32, (SUBLANES, N), 0)

    def shifted(x, s):
        return jnp.where(ridx >= s, pltpu.roll(x, s, 0), 0.0)

    for s, row in ((1, SSM_SUB - 1), (2, ROW_A64), (4, ROW_A128)):
        mr, mi = _cmul(tr_ref[row:row + 1, :], ti_ref[row:row + 1, :], shifted(fr, s), shifted(fi, s))
        fr, fi = fr + mr, fi + mi
    hr = hr_ref[...]
    hi = hi_ref[...]
    cr, ci = _cmul(tr_ref[ROW_Q0:ROW_Q8, :], ti_ref[ROW_Q0:ROW_Q8, :], hr, hi)
    cr = cr + shifted(fr, 1)
    ci = ci + shifted(fi, 1)
    nhr, nhi = _cmul(tr_ref[ROW_Q8:ROW_Q8 + 1, :], ti_ref[ROW_Q8:ROW_Q8 + 1, :], hr, hi)
    hr_ref[...] = nhr + fr[SUBLANES - 1:, :]
    hi_ref[...] = nhi + fi[SUBLANES - 1:, :]

    def fix_step(k, _):
        rows = pl.ds(pl.multiple_of(k * SUBLANES, SUBLANES), SUBLANES)
        pr = tr_ref[pl.ds(k, 1), :]
        pi = ti_ref[pl.ds(k, 1), :]
        st_ref[rows, :N] = st_ref[rows, :N] + (pr * cr - pi * ci)
        st_ref[rows, N:] = st_ref[rows, N:] + (pr * ci + pi * cr)
        return 0

    lax.fori_loop(0, SSM_SUB, fix_step, 0)

    y_perm = _dot(st_ref[...].astype(BF16), cbig_ref[...])
    y_hi = y_perm.astype(BF16)
    y_lo = (y_perm - y_hi.astype(F32)).astype(BF16)
    y = _dot(permt_ref[...], y_hi) + _dot(permt_ref[...], y_lo) + d_ref[...] * u
    g = jax.nn.gelu(y)
    z = _dot(g.astype(BF16), wglu_ref[...]) + bglu_ref[...]
    out = g * jax.nn.sigmoid(z)
    y_ref[0] = _rms(out, gout_ref[...]).astype(BF16)


def _s5(u, perm, permt, bbig, cbig, tab_r, tab_i, d_row, wglu, bglu, gout):
    B, S, W = u.shape
    T = SSM_TILE
    consts = [perm, permt, bbig, cbig, tab_r, tab_i, d_row, wglu, bglu, gout]
    return pl.pallas_call(
        _s5_kernel,
        grid=(B, S // T),
        in_specs=[pl.BlockSpec((1, T, W), lambda b, i: (b, i, 0))] + [_const_spec(c.shape) for c in consts],
        out_specs=pl.BlockSpec((1, T, W), lambda b, i: (b, i, 0)),
        out_shape=jax.ShapeDtypeStruct((B, S, W), BF16),
        scratch_shapes=[
            pltpu.VMEM((T, 2 * SSM_STATES), F32),
            pltpu.VMEM((1, SSM_STATES), F32),
            pltpu.VMEM((1, SSM_STATES), F32),
        ],
        compiler_params=pltpu.CompilerParams(
            dimension_semantics=("parallel", "arbitrary"), vmem_limit_bytes=VMEM_LIMIT),
        name="s5",
    )(u, *consts)


def _memkv_kernel(mem_ref, gmem_ref, wkv_ref, gk_ref, kt_ref, v_ref):
    m = _rms(mem_ref[0], gmem_ref[...]).astype(BF16)
    kv = _dot(m, wkv_ref[...])
    gk = gk_ref[...]
    for hh in range(XATTN_HEADS):
        kh = kv[:, hh * XATTN_HEAD_DIM:(hh + 1) * XATTN_HEAD_DIM]
        kt_ref[0, hh] = _rms(kh, gk).T.astype(BF16)
    v_ref[0] = kv[:, XATTN_WIDTH:].astype(BF16)


def _memkv(mem, gmem, wkv, gk):
    B, M, D = mem.shape
    return pl.pallas_call(
        _memkv_kernel,
        grid=(B,),
        in_specs=[pl.BlockSpec((1, M, D), lambda b: (b, 0, 0)), _const_spec(gmem.shape),
                  _const_spec(wkv.shape), _const_spec(gk.shape)],
        out_specs=[pl.BlockSpec((1, XATTN_HEADS, XATTN_HEAD_DIM, M), lambda b: (b, 0, 0, 0)),
                   pl.BlockSpec((1, M, XATTN_WIDTH), lambda b: (b, 0, 0))],
        out_shape=[jax.ShapeDtypeStruct((B, XATTN_HEADS, XATTN_HEAD_DIM, M), BF16),
                   jax.ShapeDtypeStruct((B, M, XATTN_WIDTH), BF16)],
        compiler_params=pltpu.CompilerParams(dimension_semantics=("parallel",)),
        name="memkv",
    )(mem, gmem, wkv, gk)


def _back_kernel(x1_ref, omla_ref, yssm_ref, kt_ref, vm_ref, gomla_ref, wo_ref, gx_ref, wq_ref, gqn_ref,
                 wxo_ref, g2_ref, wg_ref, wu_ref, wd_ref, out_ref):
    y_mla = _rms(omla_ref[0], gomla_ref[...]).astype(BF16)
    y = jnp.concatenate([y_mla, yssm_ref[0]], axis=-1)
    x2 = x1_ref[0] + _dot(y, wo_ref[...])
    hq = _rms(x2, gx_ref[...]).astype(BF16)
    q = _dot(hq, wq_ref[...])
    scale = XATTN_HEAD_DIM ** -0.5
    heads = []
    for hh in range(XATTN_HEADS):
        lo = hh * XATTN_HEAD_DIM
        qh = (_rms(q[:, lo:lo + XATTN_HEAD_DIM], gqn_ref[...]) * scale).astype(BF16)
        s = _dot(qh, kt_ref[0, hh])
        p = jnp.exp(s - jnp.max(s, axis=-1, keepdims=True))
        p = p / jnp.sum(p, axis=-1, keepdims=True)
        heads.append(_dot(p.astype(BF16), vm_ref[0, :, lo:lo + XATTN_HEAD_DIM]))
    o = jnp.concatenate(heads, axis=-1).astype(BF16)
    x3 = x2 + _dot(o, wxo_ref[...])
    out_ref[0] = _ffn_half_step(x3, g2_ref, wg_ref, wu_ref, wd_ref)


def _back(x1, omla, yssm, kt, vm, gomla, wo, gx, wq, gqn, wxo, g2, wg, wu, wd):
    B, S, D = x1.shape
    T = TOKEN_TILE
    M = vm.shape[1]
    row = lambda w: pl.BlockSpec((1, T, w), lambda b, i: (b, i, 0))
    consts = [gomla, wo, gx, wq, gqn, wxo, g2, wg, wu, wd]
    return pl.pallas_call(
        _back_kernel,
        grid=(B, S // T),
        in_specs=[row(D), row(MLA_WIDTH), row(SSM_WIDTH),
                  pl.BlockSpec((1, XATTN_HEADS, XATTN_HEAD_DIM, M), lambda b, i: (b, 0, 0, 0)),
                  pl.BlockSpec((1, M, XATTN_WIDTH), lambda b, i: (b, 0, 0))]
                 + [_const_spec(c.shape) for c in consts],
        out_specs=row(D),
        out_shape=jax.ShapeDtypeStruct((B, S, D), F32),
        compiler_params=pltpu.CompilerParams(
            dimension_semantics=("parallel", "parallel"), vmem_limit_bytes=VMEM_LIMIT),
        name="back",
    )(x1, omla, yssm, kt, vm, *consts)


def _row(v):
    return v.reshape(1, -1).astype(F32)


def _ffn_weights(w_gate, w_up, w_down):
    pad = D_FF_PAD - D_FF
    return (jnp.pad(w_gate, ((0, 0), (0, pad))).astype(BF16),
            jnp.pad(w_up, ((0, 0), (0, pad))).astype(BF16),
            jnp.pad(w_down, ((0, pad), (0, 0))).astype(BF16))


def _pad_rope_cols(w):
    half = MLA_ROPE // 2
    z = jnp.zeros(w.shape[:-1] + (half,), w.dtype)
    return jnp.concatenate([w[..., :half], z, w[..., half:], z], axis=-1)


def _pad_head_cols(w):
    return jnp.concatenate([w[..., :MLA_NOPE], _pad_rope_cols(w[..., MLA_NOPE:])], axis=-1)


def _block_diag(blocks):
    G, r, c = blocks.shape
    eye = jnp.eye(G, dtype=blocks.dtype)
    return jnp.einsum('grc,gk->grkc', blocks, eye).reshape(G * r, G * c)


def kernel(x, mem, positions, ffn1_norm, ffn1_w_gate, ffn1_w_up, ffn1_w_down, mix_norm, w_in, mla_q_norm, mla_w_uq, mla_kv_norm, mla_w_ukv, mla_qk_norm_q, mla_qk_norm_k, ssm_a_re, ssm_a_im, ssm_log_dt, ssm_b_re, ssm_b_im, ssm_c_re, ssm_c_im, ssm_d, ssm_w_glu, ssm_b_glu, out_norm_mla, out_norm_ssm, w_o, xattn_norm, mem_norm, xattn_w_q, xattn_w_kv, xattn_q_norm, xattn_k_norm, xattn_w_o, ffn2_norm, ffn2_w_gate, ffn2_w_up, ffn2_w_down):
    depth = ffn1_norm.shape[0]
    B, S, _ = x.shape
    pos3 = positions.reshape(B, S, 1)
    half = MLA_ROPE // 2
    inv = ROPE_THETA ** (-jnp.arange(half, dtype=F32) / half)
    zeros = jnp.zeros((half,), F32)
    ones = jnp.ones((half,), F32)
    inv_tab = jnp.concatenate([inv, zeros, inv, zeros]).reshape(1, ROPE_PAD)
    sgn_tab = jnp.concatenate([-ones, zeros, ones, zeros]).reshape(1, ROPE_PAD)
    j = jnp.arange(SSM_TILE)
    src = (j % SUBLANES) * SSM_SUB + j // SUBLANES
    perm = (src[:, None] == j[None, :]).astype(BF16)
    permt = perm.T

    for l in range(depth):
        f1 = _ffn_weights(ffn1_w_gate[l], ffn1_w_up[l], ffn1_w_down[l])
        f2 = _ffn_weights(ffn2_w_gate[l], ffn2_w_up[l], ffn2_w_down[l])
        o_kr = MLA_Q_RANK + MLA_KV_RANK
        win = jnp.concatenate([w_in[l][:, :o_kr], _pad_rope_cols(w_in[l][:, o_kr:o_kr + MLA_ROPE]),
                               w_in[l][:, o_kr + MLA_ROPE:]], axis=-1).astype(BF16)
        wuq = _pad_head_cols(mla_w_uq[l].reshape(MLA_Q_RANK, MLA_HEADS, MLA_QK)).reshape(
            MLA_Q_RANK, MLA_HEADS * HEAD_PAD).astype(BF16)
        gqk_q = _row(_pad_head_cols(mla_qk_norm_q[l]))
        gqk_k = _row(_pad_head_cols(mla_qk_norm_k[l]))

        x1, q, k, v, u = _front(
            x, pos3, _row(ffn1_norm[l]), *f1, _row(mix_norm[l]), win, _row(mla_q_norm[l]), wuq,
            _row(mla_kv_norm[l]), mla_w_ukv[l].astype(BF16), gqk_q, gqk_k, inv_tab, sgn_tab)

        o_mla = _attention(q, k, v)

        rows = lambda a: a.reshape(1, SSM_STATES).astype(F32)
        ldt = jnp.repeat(ssm_log_dt[l].astype(F32), SSM_STATE).reshape(1, SSM_STATES)
        bt = lambda b: b.astype(F32).transpose(2, 0, 1).reshape(SSM_GROUP, SSM_STATES)
        tab_r, tab_i, bb_r, bb_i = _s5prep(rows(ssm_a_re[l]), rows(ssm_a_im[l]), ldt, bt(ssm_b_re[l]), bt(ssm_b_im[l]))
        blocks = lambda m: m.reshape(SSM_GROUP, SSM_GROUPS, SSM_STATE).transpose(1, 0, 2)
        bbig = jnp.concatenate([_block_diag(blocks(bb_r)), _block_diag(blocks(bb_i))], axis=-1).astype(BF16)
        c_blocks = lambda c: c.astype(F32).transpose(0, 2, 1)
        cbig = jnp.concatenate([_block_diag(c_blocks(ssm_c_re[l])), -_block_diag(c_blocks(ssm_c_im[l]))],
                               axis=0).astype(BF16)
        y_ssm = _s5(u, perm, permt, bbig, cbig, tab_r, tab_i, _row(ssm_d[l]), ssm_w_glu[l].astype(BF16),
                    _row(ssm_b_glu[l]), _row(out_norm_ssm[l]))

        kt, vm = _memkv(mem, _row(mem_norm[l]), xattn_w_kv[l].astype(BF16), _row(xattn_k_norm[l]))

        x = _back(x1, o_mla, y_ssm, kt, vm, _row(out_norm_mla[l]), w_o[l].astype(BF16), _row(xattn_norm[l]),
                  xattn_w_q[l].astype(BF16), _row(xattn_q_norm[l]), xattn_w_o[l].astype(BF16),
                  _row(ffn2_norm[l]), *f2)
    return x
```

```python
import functools
import math

import jax
import jax.numpy as jnp
from jax import lax
from jax.experimental import pallas as pl
from jax.experimental.pallas import tpu as pltpu

F32 = jnp.float32
BF16 = jnp.bfloat16

D_MODEL = 1024
D_FF = 2752
MLA_HEADS = 4
MLA_Q_RANK = 384
MLA_KV_RANK = 256
MLA_NOPE = 128
MLA_ROPE = 64
MLA_V = 128
MLA_QK = MLA_NOPE + MLA_ROPE
MLA_WIDTH = MLA_HEADS * MLA_V
SSM_WIDTH = D_MODEL - MLA_WIDTH
SSM_GROUP = 16
SSM_GROUPS = SSM_WIDTH // SSM_GROUP
SSM_STATE = 64
XATTN_HEADS = 4
XATTN_HEAD_DIM = 128
XATTN_WIDTH = XATTN_HEADS * XATTN_HEAD_DIM
ROPE_THETA = 10000.0
EPS = 1e-6

LANES = 128
SUBLANES = 8
D_FF_PAD = 2816
FF_CHUNK = D_FF_PAD // 2
HEAD_PAD = 2 * LANES
ROPE_PAD = LANES
IN_PAD = MLA_Q_RANK + MLA_KV_RANK + ROPE_PAD + SSM_WIDTH
TOKEN_TILE = 512
ATTN_TILE = 512
ATTN_HEADS_PER_STEP = 4
LOG2_E = math.log2(math.e)
SSM_STATES = SSM_GROUPS * SSM_STATE
SSM_TILE = 256
SSM_SUB = SSM_TILE // SUBLANES
TABLE_ROWS = 48
ROW_Q0 = SSM_SUB
ROW_Q8 = ROW_Q0 + SUBLANES
ROW_A64 = ROW_Q8 + 1
ROW_A128 = ROW_Q8 + 2
VMEM_LIMIT = 56 * 1024 * 1024


def _rms(x, g):
    return x * lax.rsqrt(jnp.mean(x * x, axis=-1, keepdims=True) + EPS) * g


def _dot(a, b):
    return jnp.dot(a, b, preferred_element_type=F32)


def _ffn_half_step(x, g_ref, wg_ref, wu_ref, wd_ref):
    h = _rms(x, g_ref[...]).astype(BF16)
    acc = None
    for c in range(D_FF_PAD // FF_CHUNK):
        lo = c * FF_CHUNK
        gate = _dot(h, wg_ref[:, lo:lo + FF_CHUNK])
        up = _dot(h, wu_ref[:, lo:lo + FF_CHUNK])
        act = (gate * jax.nn.sigmoid(gate) * up).astype(BF16)
        part = _dot(act, wd_ref[lo:lo + FF_CHUNK, :])
        acc = part if acc is None else acc + part
    return x + 0.5 * acc


def _rope(x, cos, sin_signed):
    return x * cos + pltpu.roll(x, ROPE_PAD // 2, 1) * sin_signed


def _front_kernel(x_ref, pos_ref, g1_ref, wg_ref, wu_ref, wd_ref, gmix_ref, win_ref,
                  gq_ref, wuq_ref, gkv_ref, wukv_ref, gqk_q_ref, gqk_k_ref, inv_ref, sgn_ref,
                  x1_ref, q_ref, k_ref, v_ref, u_ref):
    x1 = _ffn_half_step(x_ref[0], g1_ref, wg_ref, wu_ref, wd_ref)
    x1_ref[0] = x1
    h = _rms(x1, gmix_ref[...]).astype(BF16)
    proj = _dot(h, win_ref[...])
    o_ckv = MLA_Q_RANK
    o_kr = o_ckv + MLA_KV_RANK
    o_u = o_kr + ROPE_PAD
    u_ref[0] = proj[:, o_u:]
    cq = _rms(proj[:, :o_ckv], gq_ref[...]).astype(BF16)
    q_raw = _dot(cq, wuq_ref[...])
    ckv = _rms(proj[:, o_ckv:o_kr], gkv_ref[...]).astype(BF16)
    kv = _dot(ckv, wukv_ref[...])
    kr = proj[:, o_kr:o_u]
    ang = pos_ref[0].astype(F32) * inv_ref[...]
    cos = jnp.cos(ang)
    sin = jnp.sin(ang) * sgn_ref[...]
    kr_ss = jnp.sum(kr * kr, axis=-1, keepdims=True)
    gq = gqk_q_ref[...]
    gk = gqk_k_ref[...]
    inv_qk = 1.0 / MLA_QK
    scale = MLA_QK ** -0.5 * LOG2_E
    for hh in range(MLA_HEADS):
        qh = q_raw[:, hh * HEAD_PAD:(hh + 1) * HEAD_PAD]
        qn = qh * lax.rsqrt(jnp.sum(qh * qh, axis=-1, keepdims=True) * inv_qk + EPS) * gq
        q_rot = _rope(qn[:, MLA_NOPE:], cos, sin)
        q_ref[0, hh] = (jnp.concatenate([qn[:, :MLA_NOPE], q_rot], axis=-1) * scale).astype(BF16)
        kn = kv[:, hh * HEAD_PAD:hh * HEAD_PAD + MLA_NOPE]
        vv = kv[:, hh * HEAD_PAD + MLA_NOPE:(hh + 1) * HEAD_PAD]
        inv_k = lax.rsqrt((jnp.sum(kn * kn, axis=-1, keepdims=True) + kr_ss) * inv_qk + EPS)
        k_rot = _rope(kr * inv_k * gk[:, MLA_NOPE:], cos, sin)
        k_ref[0, hh] = jnp.concatenate([kn * inv_k * gk[:, :MLA_NOPE], k_rot], axis=-1).astype(BF16)
        v_ref[0, hh] = vv.astype(BF16)


def _const_spec(shape):
    return pl.BlockSpec(shape, lambda *_: (0,) * len(shape), pipeline_mode=pl.Buffered(1))


def _front(x, pos3, g1, wg, wu, wd, gmix, win, gq, wuq, gkv, wukv, gqk_q, gqk_k, inv_tab, sgn_tab):
    B, S, D = x.shape
    T = TOKEN_TILE
    row = lambda w: pl.BlockSpec((1, T, w), lambda b, i: (b, i, 0))
    head = lambda w: pl.BlockSpec((1, MLA_HEADS, T, w), lambda b, i: (b, 0, i, 0))
    consts = [g1, wg, wu, wd, gmix, win, gq, wuq, gkv, wukv, gqk_q, gqk_k, inv_tab, sgn_tab]
    return pl.pallas_call(
        _front_kernel,
        grid=(B, S // T),
        in_specs=[row(D), row(1)] + [_const_spec(c.shape) for c in consts],
        out_specs=[row(D), head(HEAD_PAD), head(HEAD_PAD), head(MLA_V), row(SSM_WIDTH)],
        out_shape=[
            jax.ShapeDtypeStruct((B, S, D), F32),
            jax.ShapeDtypeStruct((B, MLA_HEADS, S, HEAD_PAD), BF16),
            jax.ShapeDtypeStruct((B, MLA_HEADS, S, HEAD_PAD), BF16),
            jax.ShapeDtypeStruct((B, MLA_HEADS, S, MLA_V), BF16),
            jax.ShapeDtypeStruct((B, S, SSM_WIDTH), F32),
        ],
        compiler_params=pltpu.CompilerParams(
            dimension_semantics=("parallel", "parallel"), vmem_limit_bytes=VMEM_LIMIT),
        name="front",
    )(x, pos3, *consts)


def _attn_kernel(q_ref, k_ref, v_ref, o_ref, m_ref, acc_ref, alpha_a, p_a, alpha_b, p_b):
    T = ATTN_TILE
    G = ATTN_HEADS_PER_STEP
    qi = pl.program_id(2)
    m_ref[...] = jnp.full(m_ref.shape, -jnp.inf, F32)
    acc_ref[...] = jnp.zeros(acc_ref.shape, F32)
    ones = jnp.ones((T, LANES), BF16)

    def tile_rows(j):
        return pl.ds(pl.multiple_of(j * T, T), T)

    def scores(j):
        return [lax.dot_general(q_ref[0, g], k_ref[0, g, tile_rows(j), :], (((1,), (1,)), ((), ())),
                                preferred_element_type=F32) for g in range(G)]

    def softmax(s_all, alpha_ref, p_ref, masked):
        for g, s in enumerate(s_all):
            if masked:
                rows = lax.broadcasted_iota(jnp.int32, (T, T), 0)
                cols = lax.broadcasted_iota(jnp.int32, (T, T), 1)
                s = jnp.where(rows >= cols, s, -jnp.inf)
            m_prev = m_ref[g]
            m_new = jnp.maximum(m_prev, jnp.max(s, axis=-1, keepdims=True))
            alpha_ref[g] = jnp.exp2(m_prev - m_new)
            p_ref[g] = jnp.exp2(s - pltpu.repeat(m_new, T // LANES, 1)).astype(BF16)
            m_ref[g] = m_new

    def values(j, alpha_ref, p_ref):
        for g in range(G):
            v_ext = jnp.concatenate([v_ref[0, g, tile_rows(j), :], ones], axis=-1)
            acc_ref[g] = pltpu.repeat(alpha_ref[g], 2, 1) * acc_ref[g] + _dot(p_ref[g], v_ext)

    def advance(j, prev, prev_bufs, next_bufs):
        s_all = scores(j)
        values(prev, *prev_bufs)
        softmax(s_all, *next_bufs, masked=False)

    buf_a = (alpha_a, p_a)
    buf_b = (alpha_b, p_b)
    softmax(scores(qi), *buf_a, masked=True)

    def pair(i, prev):
        advance(2 * i, prev, buf_a, buf_b)
        advance(2 * i + 1, 2 * i, buf_b, buf_a)
        return 2 * i + 1

    prev = lax.fori_loop(0, qi // 2, pair, qi)

    @pl.when(qi % 2 == 1)
    def _():
        advance(qi - 1, prev, buf_a, buf_b)
        values(qi - 1, *buf_b)

    @pl.when(qi % 2 == 0)
    def _():
        values(prev, *buf_a)

    for g in range(G):
        acc = acc_ref[g]
        o_ref[0, :, g * MLA_V:(g + 1) * MLA_V] = acc[:, :MLA_V] / acc[:, MLA_V:]


def _attention(q, k, v):
    B, H, S, _ = q.shape
    T = ATTN_TILE
    G = ATTN_HEADS_PER_STEP
    return pl.pallas_call(
        _attn_kernel,
        grid=(B, H // G, S // T),
        in_specs=[
            pl.BlockSpec((1, G, T, HEAD_PAD), lambda b, h, i: (b, h, i, 0)),
            pl.BlockSpec((1, G, S, HEAD_PAD), lambda b, h, i: (b, h, 0, 0)),
            pl.BlockSpec((1, G, S, MLA_V), lambda b, h, i: (b, h, 0, 0)),
        ],
        out_specs=pl.BlockSpec((1, T, G * MLA_V), lambda b, h, i: (b, i, h)),
        out_shape=jax.ShapeDtypeStruct((B, S, H * MLA_V), F32),
        scratch_shapes=[pltpu.VMEM((G, T, LANES), F32), pltpu.VMEM((G, T, 2 * MLA_V), F32),
                        pltpu.VMEM((G, T, LANES), F32), pltpu.VMEM((G, T, T), BF16),
                        pltpu.VMEM((G, T, LANES), F32), pltpu.VMEM((G, T, T), BF16)],
        compiler_params=pltpu.CompilerParams(
            dimension_semantics=("parallel", "parallel", "parallel"), vmem_limit_bytes=VMEM_LIMIT),
        name="attn",
    )(q, k, v)


def _cmul(ar, ai, br, bi):
    return ar * br - ai * bi, ar * bi + ai * br


def _s5prep_kernel(lr_ref, li_ref, ldt_ref, btr_ref, bti_ref, tr_ref, ti_ref, bbr_ref, bbi_ref):
    lr = lr_ref[...]
    li = li_ref[...]
    dt = jnp.exp(ldt_ref[...])
    decay = jnp.exp(lr * dt)
    ar = decay * jnp.cos(li * dt)
    ai = decay * jnp.sin(li * dt)
    den = lr * lr + li * li
    nr = ar - 1.0
    coef_r = (nr * lr + ai * li) / den
    coef_i = (ai * lr - nr * li) / den
    br = btr_ref[...]
    bi = bti_ref[...]
    bbr_ref[...] = coef_r * br - coef_i * bi
    bbi_ref[...] = coef_r * bi + coef_i * br
    pr, pi = ar, ai
    tr_ref[0:1, :] = pr
    ti_ref[0:1, :] = pi
    for k in range(1, SSM_SUB):
        pr, pi = _cmul(pr, pi, ar, ai)
        tr_ref[k:k + 1, :] = pr
        ti_ref[k:k + 1, :] = pi
    a32r, a32i = pr, pi
    qr, qi = jnp.ones_like(ar), jnp.zeros_like(ar)
    for r in range(SUBLANES + 1):
        tr_ref[ROW_Q0 + r:ROW_Q0 + r + 1, :] = qr
        ti_ref[ROW_Q0 + r:ROW_Q0 + r + 1, :] = qi
        qr, qi = _cmul(qr, qi, a32r, a32i)
    a64r, a64i = _cmul(a32r, a32i, a32r, a32i)
    a128r, a128i = _cmul(a64r, a64i, a64r, a64i)
    tr_ref[ROW_A64:ROW_A64 + 1, :] = a64r
    ti_ref[ROW_A64:ROW_A64 + 1, :] = a64i
    tr_ref[ROW_A128:ROW_A128 + 1, :] = a128r
    ti_ref[ROW_A128:ROW_A128 + 1, :] = a128i
    zeros = jnp.zeros((TABLE_ROWS - ROW_A128 - 1, SSM_STATES), F32)
    tr_ref[ROW_A128 + 1:, :] = zeros
    ti_ref[ROW_A128 + 1:, :] = zeros


def _s5prep(lr, li, ldt, btr, bti):
    tab = jax.ShapeDtypeStruct((TABLE_ROWS, SSM_STATES), F32)
    bb = jax.ShapeDtypeStruct((SSM_GROUP, SSM_STATES), F32)
    return pl.pallas_call(_s5prep_kernel, out_shape=[tab, tab, bb, bb], name="s5prep")(lr, li, ldt, btr, bti)


def _s5_kernel(u_ref, perm_ref, permt_ref, bbig_ref, cbig_ref, tr_ref, ti_ref, d_ref, wglu_ref, bglu_ref,
               gout_ref, y_ref, st_ref, hr_ref, hi_ref):
    N = SSM_STATES

    @pl.when(pl.program_id(1) == 0)
    def _():
        hr_ref[...] = jnp.zeros_like(hr_ref)
        hi_ref[...] = jnp.zeros_like(hi_ref)

    u = u_ref[0]
    u_perm = _dot(perm_ref[...], u.astype(BF16)).astype(BF16)
    st_ref[...] = _dot(u_perm, bbig_ref[...])

    ar = tr_ref[0:1, :]
    ai = ti_ref[0:1, :]

    def local_step(k, carry):
        xr, xi = carry
        rows = pl.ds(pl.multiple_of(k * SUBLANES, SUBLANES), SUBLANES)
        nr = ar * xr - ai * xi + st_ref[rows, :N]
        ni = ar * xi + ai * xr + st_ref[rows, N:]
        st_ref[rows, :N] = nr
        st_ref[rows, N:] = ni
        return nr, ni

    zero = jnp.zeros((SUBLANES, N), F32)
    fr, fi = lax.fori_loop(0, SSM_SUB, local_step, (zero, zero))

    ridx = lax.broadcasted_iota(jnp.int32, (SUBLANES, N), 0)

    def shifted(x, s):
        return jnp.where(ridx >= s, pltpu.roll(x, s, 0), 0.0)

    for s, row in ((1, SSM_SUB - 1), (2, ROW_A64), (4, ROW_A128)):
        mr, mi = _cmul(tr_ref[row:row + 1, :], ti_ref[row:row + 1, :], shifted(fr, s), shifted(fi, s))
        fr, fi = fr + mr, fi + mi
    hr = hr_ref[...]
    hi = hi_ref[...]
    cr, ci = _cmul(tr_ref[ROW_Q0:ROW_Q8, :], ti_ref[ROW_Q0:ROW_Q8, :], hr, hi)
    cr = cr + shifted(fr, 1)
    ci = ci + shifted(fi, 1)
    nhr, nhi = _cmul(tr_ref[ROW_Q8:ROW_Q8 + 1, :], ti_ref[ROW_Q8:ROW_Q8 + 1, :], hr, hi)
    hr_ref[...] = nhr + fr[SUBLANES - 1:, :]
    hi_ref[...] = nhi + fi[SUBLANES - 1:, :]

    def fix_step(k, _):
        rows = pl.ds(pl.multiple_of(k * SUBLANES, SUBLANES), SUBLANES)
        pr = tr_ref[pl.ds(k, 1), :]
        pi = ti_ref[pl.ds(k, 1), :]
        st_ref[rows, :N] = st_ref[rows, :N] + (pr * cr - pi * ci)
        st_ref[rows, N:] = st_ref[rows, N:] + (pr * ci + pi * cr)
        return 0

    lax.fori_loop(0, SSM_SUB, fix_step, 0)

    y_perm = _dot(st_ref[...].astype(BF16), cbig_ref[...])
    y_hi = y_perm.astype(BF16)
    y_lo = (y_perm - y_hi.astype(F32)).astype(BF16)
    y = _dot(permt_ref[...], y_hi) + _dot(permt_ref[...], y_lo) + d_ref[...] * u
    g = jax.nn.gelu(y)
    z = _dot(g.astype(BF16), wglu_ref[...]) + bglu_ref[...]
    out = g * jax.nn.sigmoid(z)
    y_ref[0] = _rms(out, gout_ref[...]).astype(BF16)


def _s5(u, perm, permt, bbig, cbig, tab_r, tab_i, d_row, wglu, bglu, gout):
    B, S, W = u.shape
    T = SSM_TILE
    consts = [perm, permt, bbig, cbig, tab_r, tab_i, d_row, wglu, bglu, gout]
    return pl.pallas_call(
        _s5_kernel,
        grid=(B, S // T),
        in_specs=[pl.BlockSpec((1, T, W), lambda b, i: (b, i, 0))] + [_const_spec(c.shape) for c in consts],
        out_specs=pl.BlockSpec((1, T, W), lambda b, i: (b, i, 0)),
        out_shape=jax.ShapeDtypeStruct((B, S, W), BF16),
        scratch_shapes=[
            pltpu.VMEM((T, 2 * SSM_STATES), F32),
            pltpu.VMEM((1, SSM_STATES), F32),
            pltpu.VMEM((1, SSM_STATES), F32),
        ],
        compiler_params=pltpu.CompilerParams(
            dimension_semantics=("parallel", "arbitrary"), vmem_limit_bytes=VMEM_LIMIT),
        name="s5",
    )(u, *consts)


def _memkv_kernel(mem_ref, gmem_ref, wkv_ref, gk_ref, kt_ref, v_ref):
    m = _rms(mem_ref[0], gmem_ref[...]).astype(BF16)
    kv = _dot(m, wkv_ref[...])
    gk = gk_ref[...]
    for hh in range(XATTN_HEADS):
        kh = kv[:, hh * XATTN_HEAD_DIM:(hh + 1) * XATTN_HEAD_DIM]
        kt_ref[0, hh] = _rms(kh, gk).T.astype(BF16)
    v_ref[0] = kv[:, XATTN_WIDTH:].astype(BF16)


def _memkv(mem, gmem, wkv, gk):
    B, M, D = mem.shape
    return pl.pallas_call(
        _memkv_kernel,
        grid=(B,),
        in_specs=[pl.BlockSpec((1, M, D), lambda b: (b, 0, 0)), _const_spec(gmem.shape),
                  _const_spec(wkv.shape), _const_spec(gk.shape)],
        out_specs=[pl.BlockSpec((1, XATTN_HEADS, XATTN_HEAD_DIM, M), lambda b: (b, 0, 0, 0)),
                   pl.BlockSpec((1, M, XATTN_WIDTH), lambda b: (b, 0, 0))],
        out_shape=[jax.ShapeDtypeStruct((B, XATTN_HEADS, XATTN_HEAD_DIM, M), BF16),
                   jax.ShapeDtypeStruct((B, M, XATTN_WIDTH), BF16)],
        compiler_params=pltpu.CompilerParams(dimension_semantics=("parallel",)),
        name="memkv",
    )(mem, gmem, wkv, gk)


def _back_kernel(x1_ref, omla_ref, yssm_ref, kt_ref, vm_ref, gomla_ref, wo_ref, gx_ref, wq_ref, gqn_ref,
                 wxo_ref, g2_ref, wg_ref, wu_ref, wd_ref, out_ref):
    y_mla = _rms(omla_ref[0], gomla_ref[...]).astype(BF16)
    y = jnp.concatenate([y_mla, yssm_ref[0]], axis=-1)
    x2 = x1_ref[0] + _dot(y, wo_ref[...])
    hq = _rms(x2, gx_ref[...]).astype(BF16)
    q = _dot(hq, wq_ref[...])
    scale = XATTN_HEAD_DIM ** -0.5
    heads = []
    for hh in range(XATTN_HEADS):
        lo = hh * XATTN_HEAD_DIM
        qh = (_rms(q[:, lo:lo + XATTN_HEAD_DIM], gqn_ref[...]) * scale).astype(BF16)
        s = _dot(qh, kt_ref[0, hh])
        p = jnp.exp(s - jnp.max(s, axis=-1, keepdims=True))
        p = p / jnp.sum(p, axis=-1, keepdims=True)
        heads.append(_dot(p.astype(BF16), vm_ref[0, :, lo:lo + XATTN_HEAD_DIM]))
    o = jnp.concatenate(heads, axis=-1).astype(BF16)
    x3 = x2 + _dot(o, wxo_ref[...])
    out_ref[0] = _ffn_half_step(x3, g2_ref, wg_ref, wu_ref, wd_ref)


def _back(x1, omla, yssm, kt, vm, gomla, wo, gx, wq, gqn, wxo, g2, wg, wu, wd):
    B, S, D = x1.shape
    T = TOKEN_TILE
    M = vm.shape[1]
    row = lambda w: pl.BlockSpec((1, T, w), lambda b, i: (b, i, 0))
    consts = [gomla, wo, gx, wq, gqn, wxo, g2, wg, wu, wd]
    return pl.pallas_call(
        _back_kernel,
        grid=(B, S // T),
        in_specs=[row(D), row(MLA_WIDTH), row(SSM_WIDTH),
                  pl.BlockSpec((1, XATTN_HEADS, XATTN_HEAD_DIM, M), lambda b, i: (b, 0, 0, 0)),
                  pl.BlockSpec((1, M, XATTN_WIDTH), lambda b, i: (b, 0, 0))]
                 + [_const_spec(c.shape) for c in consts],
        out_specs=row(D),
        out_shape=jax.ShapeDtypeStruct((B, S, D), F32),
        compiler_params=pltpu.CompilerParams(
            dimension_semantics=("parallel", "parallel"), vmem_limit_bytes=VMEM_LIMIT),
        name="back",
    )(x1, omla, yssm, kt, vm, *consts)


def _row(v):
    return v.reshape(1, -1).astype(F32)


def _ffn_weights(w_gate, w_up, w_down):
    pad = D_FF_PAD - D_FF
    return (jnp.pad(w_gate, ((0, 0), (0, pad))).astype(BF16),
            jnp.pad(w_up, ((0, 0), (0, pad))).astype(BF16),
            jnp.pad(w_down, ((0, pad), (0, 0))).astype(BF16))


def _pad_rope_cols(w):
    half = MLA_ROPE // 2
    z = jnp.zeros(w.shape[:-1] + (half,), w.dtype)
    return jnp.concatenate([w[..., :half], z, w[..., half:], z], axis=-1)


def _pad_head_cols(w):
    return jnp.concatenate([w[..., :MLA_NOPE], _pad_rope_cols(w[..., MLA_NOPE:])], axis=-1)


def _block_diag(blocks):
    G, r, c = blocks.shape
    eye = jnp.eye(G, dtype=blocks.dtype)
    return jnp.einsum('grc,gk->grkc', blocks, eye).reshape(G * r, G * c)


def kernel(x, mem, positions, ffn1_norm, ffn1_w_gate, ffn1_w_up, ffn1_w_down, mix_norm, w_in, mla_q_norm, mla_w_uq, mla_kv_norm, mla_w_ukv, mla_qk_norm_q, mla_qk_norm_k, ssm_a_re, ssm_a_im, ssm_log_dt, ssm_b_re, ssm_b_im, ssm_c_re, ssm_c_im, ssm_d, ssm_w_glu, ssm_b_glu, out_norm_mla, out_norm_ssm, w_o, xattn_norm, mem_norm, xattn_w_q, xattn_w_kv, xattn_q_norm, xattn_k_norm, xattn_w_o, ffn2_norm, ffn2_w_gate, ffn2_w_up, ffn2_w_down):
    depth = ffn1_norm.shape[0]
    B, S, _ = x.shape
    pos3 = positions.reshape(B, S, 1)
    half = MLA_ROPE // 2
    inv = ROPE_THETA ** (-jnp.arange(half, dtype=F32) / half)
    zeros = jnp.zeros((half,), F32)
    ones = jnp.ones((half,), F32)
    inv_tab = jnp.concatenate([inv, zeros, inv, zeros]).reshape(1, ROPE_PAD)
    sgn_tab = jnp.concatenate([-ones, zeros, ones, zeros]).reshape(1, ROPE_PAD)
    j = jnp.arange(SSM_TILE)
    src = (j % SUBLANES) * SSM_SUB + j // SUBLANES
    perm = (src[:, None] == j[None, :]).astype(BF16)
    permt = perm.T

    for l in range(depth):
        f1 = _ffn_weights(ffn1_w_gate[l], ffn1_w_up[l], ffn1_w_down[l])
        f2 = _ffn_weights(ffn2_w_gate[l], ffn2_w_up[l], ffn2_w_down[l])
        o_kr = MLA_Q_RANK + MLA_KV_RANK
        win = jnp.concatenate([w_in[l][:, :o_kr], _pad_rope_cols(w_in[l][:, o_kr:o_kr + MLA_ROPE]),
                               w_in[l][:, o_kr + MLA_ROPE:]], axis=-1).astype(BF16)
        wuq = _pad_head_cols(mla_w_uq[l].reshape(MLA_Q_RANK, MLA_HEADS, MLA_QK)).reshape(
            MLA_Q_RANK, MLA_HEADS * HEAD_PAD).astype(BF16)
        gqk_q = _row(_pad_head_cols(mla_qk_norm_q[l]))
        gqk_k = _row(_pad_head_cols(mla_qk_norm_k[l]))

        x1, q, k, v, u = _front(
            x, pos3, _row(ffn1_norm[l]), *f1, _row(mix_norm[l]), win, _row(mla_q_norm[l]), wuq,
            _row(mla_kv_norm[l]), mla_w_ukv[l].astype(BF16), gqk_q, gqk_k, inv_tab, sgn_tab)

        o_mla = _attention(q, k, v)

        rows = lambda a: a.reshape(1, SSM_STATES).astype(F32)
        ldt = jnp.repeat(ssm_log_dt[l].astype(F32), SSM_STATE).reshape(1, SSM_STATES)
        bt = lambda b: b.astype(F32).transpose(2, 0, 1).reshape(SSM_GROUP, SSM_STATES)
        tab_r, tab_i, bb_r, bb_i = _s5prep(rows(ssm_a_re[l]), rows(ssm_a_im[l]), ldt, bt(ssm_b_re[l]), bt(ssm_b_im[l]))
        blocks = lambda m: m.reshape(SSM_GROUP, SSM_GROUPS, SSM_STATE).transpose(1, 0, 2)
        bbig = jnp.concatenate([_block_diag(blocks(bb_r)), _block_diag(blocks(bb_i))], axis=-1).astype(BF16)
        c_blocks = lambda c: c.astype(F32).transpose(0, 2, 1)
        cbig = jnp.concatenate([_block_diag(c_blocks(ssm_c_re[l])), -_block_diag(c_blocks(ssm_c_im[l]))],
                               axis=0).astype(BF16)
        y_ssm = _s5(u, perm, permt, bbig, cbig, tab_r, tab_i, _row(ssm_d[l]), ssm_w_glu[l].astype(BF16),
                    _row(ssm_b_glu[l]), _row(out_norm_ssm[l]))

        kt, vm = _memkv(mem, _row(mem_norm[l]), xattn_w_kv[l].astype(BF16), _row(xattn_k_norm[l]))

        x = _back(x1, o_mla, y_ssm, kt, vm, _row(out_norm_mla[l]), w_o[l].astype(BF16), _row(xattn_norm[l]),
                  xattn_w_q[l].astype(BF16), _row(xattn_q_norm[l]), xattn_w_o[l].astype(BF16),
                  _row(ffn2_norm[l]), *f2)
    return x
```

```python
import functools
import math

import jax
import jax.numpy as jnp
from jax import lax
from jax.experimental import pallas as pl
from jax.experimental.pallas import tpu as pltpu

F32 = jnp.float32
BF16 = jnp.bfloat16

D_MODEL = 1024
D_FF = 2752
MLA_HEADS = 4
MLA_Q_RANK = 384
MLA_KV_RANK = 256
MLA_NOPE = 128
MLA_ROPE = 64
MLA_V = 128
MLA_QK = MLA_NOPE + MLA_ROPE
MLA_WIDTH = MLA_HEADS * MLA_V
SSM_WIDTH = D_MODEL - MLA_WIDTH
SSM_GROUP = 16
SSM_GROUPS = SSM_WIDTH // SSM_GROUP
SSM_STATE = 64
XATTN_HEADS = 4
XATTN_HEAD_DIM = 128
XATTN_WIDTH = XATTN_HEADS * XATTN_HEAD_DIM
ROPE_THETA = 10000.0
EPS = 1e-6

LANES = 128
SUBLANES = 8
D_FF_PAD = 2816
FF_CHUNK = D_FF_PAD // 2
HEAD_PAD = 2 * LANES
ROPE_PAD = LANES
IN_PAD = MLA_Q_RANK + MLA_KV_RANK + ROPE_PAD + SSM_WIDTH
TOKEN_TILE = 512
ATTN_TILE = 512
ATTN_HEADS_PER_STEP = 4
LOG2_E = math.log2(math.e)
SSM_STATES = SSM_GROUPS * SSM_STATE
SSM_TILE = 256
SSM_SUB = SSM_TILE // SUBLANES
SSM_BLOCKS = 2
SSM_HALF = SSM_STATES // SSM_BLOCKS
TABLE_ROWS = 16
ROW_Q8 = SUBLANES
ROW_A32 = ROW_Q8 + 1
ROW_A64 = ROW_Q8 + 2
ROW_A128 = ROW_Q8 + 3
VMEM_LIMIT = 56 * 1024 * 1024


def _rms(x, g):
    return x * lax.rsqrt(jnp.mean(x * x, axis=-1, keepdims=True) + EPS) * g


def _dot(a, b):
    return jnp.dot(a, b, preferred_element_type=F32)


def _ffn_half_step(x, g_ref, wg_ref, wu_ref, wd_ref):
    h = _rms(x, g_ref[...]).astype(BF16)
    acc = None
    for c in range(D_FF_PAD // FF_CHUNK):
        lo = c * FF_CHUNK
        gate = _dot(h, wg_ref[:, lo:lo + FF_CHUNK])
        up = _dot(h, wu_ref[:, lo:lo + FF_CHUNK])
        act = (gate * jax.nn.sigmoid(gate) * up).astype(BF16)
        part = _dot(act, wd_ref[lo:lo + FF_CHUNK, :])
        acc = part if acc is None else acc + part
    return x + 0.5 * acc


def _rope(x, cos, sin_signed):
    return x * cos + pltpu.roll(x, ROPE_PAD // 2, 1) * sin_signed


def _front_kernel(x_ref, pos_ref, g1_ref, wg_ref, wu_ref, wd_ref, gmix_ref, win_ref,
                  gq_ref, wuq_ref, gkv_ref, wukv_ref, gqk_q_ref, gqk_k_ref, inv_ref, sgn_ref,
                  x1_ref, q_ref, k_ref, v_ref, u_ref):
    x1 = _ffn_half_step(x_ref[0], g1_ref, wg_ref, wu_ref, wd_ref)
    x1_ref[0] = x1
    h = _rms(x1, gmix_ref[...]).astype(BF16)
    proj = _dot(h, win_ref[...])
    o_ckv = MLA_Q_RANK
    o_kr = o_ckv + MLA_KV_RANK
    o_u = o_kr + ROPE_PAD
    u_ref[0] = proj[:, o_u:]
    cq = _rms(proj[:, :o_ckv], gq_ref[...]).astype(BF16)
    q_raw = _dot(cq, wuq_ref[...])
    ckv = _rms(proj[:, o_ckv:o_kr], gkv_ref[...]).astype(BF16)
    kv = _dot(ckv, wukv_ref[...])
    kr = proj[:, o_kr:o_u]
    ang = pos_ref[0].astype(F32) * inv_ref[...]
    cos = jnp.cos(ang)
    sin = jnp.sin(ang) * sgn_ref[...]
    kr_ss = jnp.sum(kr * kr, axis=-1, keepdims=True)
    gq = gqk_q_ref[...]
    gk = gqk_k_ref[...]
    inv_qk = 1.0 / MLA_QK
    scale = MLA_QK ** -0.5 * LOG2_E
    for hh in range(MLA_HEADS):
        qh = q_raw[:, hh * HEAD_PAD:(hh + 1) * HEAD_PAD]
        qn = qh * lax.rsqrt(jnp.sum(qh * qh, axis=-1, keepdims=True) * inv_qk + EPS) * gq
        q_rot = _rope(qn[:, MLA_NOPE:], cos, sin)
        q_ref[0, hh] = (jnp.concatenate([qn[:, :MLA_NOPE], q_rot], axis=-1) * scale).astype(BF16)
        kn = kv[:, hh * HEAD_PAD:hh * HEAD_PAD + MLA_NOPE]
        vv = kv[:, hh * HEAD_PAD + MLA_NOPE:(hh + 1) * HEAD_PAD]
        inv_k = lax.rsqrt((jnp.sum(kn * kn, axis=-1, keepdims=True) + kr_ss) * inv_qk + EPS)
        k_rot = _rope(kr * inv_k * gk[:, MLA_NOPE:], cos, sin)
        k_ref[0, hh] = jnp.concatenate([kn * inv_k * gk[:, :MLA_NOPE], k_rot], axis=-1).astype(BF16)
        v_ref[0, hh] = vv.astype(BF16)


def _const_spec(shape):
    return pl.BlockSpec(shape, lambda *_: (0,) * len(shape), pipeline_mode=pl.Buffered(1))


def _front(x, pos3, g1, wg, wu, wd, gmix, win, gq, wuq, gkv, wukv, gqk_q, gqk_k, inv_tab, sgn_tab):
    B, S, D = x.shape
    T = TOKEN_TILE
    row = lambda w: pl.BlockSpec((1, T, w), lambda b, i: (b, i, 0))
    head = lambda w: pl.BlockSpec((1, MLA_HEADS, T, w), lambda b, i: (b, 0, i, 0))
    consts = [g1, wg, wu, wd, gmix, win, gq, wuq, gkv, wukv, gqk_q, gqk_k, inv_tab, sgn_tab]
    return pl.pallas_call(
        _front_kernel,
        grid=(B, S // T),
        in_specs=[row(D), row(1)] + [_const_spec(c.shape) for c in consts],
        out_specs=[row(D), head(HEAD_PAD), head(HEAD_PAD), head(MLA_V), row(SSM_WIDTH)],
        out_shape=[
            jax.ShapeDtypeStruct((B, S, D), F32),
            jax.ShapeDtypeStruct((B, MLA_HEADS, S, HEAD_PAD), BF16),
            jax.ShapeDtypeStruct((B, MLA_HEADS, S, HEAD_PAD), BF16),
            jax.ShapeDtypeStruct((B, MLA_HEADS, S, MLA_V), BF16),
            jax.ShapeDtypeStruct((B, S, SSM_WIDTH), F32),
        ],
        compiler_params=pltpu.CompilerParams(
            dimension_semantics=("parallel", "parallel"), vmem_limit_bytes=VMEM_LIMIT),
        name="front",
    )(x, pos3, *consts)


def _attn_kernel(q_ref, k_ref, v_ref, o_ref, m_ref, acc_ref, alpha_a, p_a, alpha_b, p_b):
    T = ATTN_TILE
    G = ATTN_HEADS_PER_STEP
    qi = pl.program_id(2)
    m_ref[...] = jnp.full(m_ref.shape, -jnp.inf, F32)
    acc_ref[...] = jnp.zeros(acc_ref.shape, F32)
    ones = jnp.ones((T, LANES), BF16)

    def tile_rows(j):
        return pl.ds(pl.multiple_of(j * T, T), T)

    def scores(j):
        return [lax.dot_general(q_ref[0, g], k_ref[0, g, tile_rows(j), :], (((1,), (1,)), ((), ())),
                                preferred_element_type=F32) for g in range(G)]

    def softmax(s_all, alpha_ref, p_ref, masked):
        for g, s in enumerate(s_all):
            if masked:
                rows = lax.broadcasted_iota(jnp.int32, (T, T), 0)
                cols = lax.broadcasted_iota(jnp.int32, (T, T), 1)
                s = jnp.where(rows >= cols, s, -jnp.inf)
            m_prev = m_ref[g]
            m_new = jnp.maximum(m_prev, jnp.max(s, axis=-1, keepdims=True))
            alpha_ref[g] = jnp.exp2(m_prev - m_new)
            p_ref[g] = jnp.exp2(s - jnp.tile(m_new, (1, T // LANES))).astype(BF16)
            m_ref[g] = m_new

    def values(j, alpha_ref, p_ref):
        for g in range(G):
            v_ext = jnp.concatenate([v_ref[0, g, tile_rows(j), :], ones], axis=-1)
            acc_ref[g] = jnp.tile(alpha_ref[g], (1, 2)) * acc_ref[g] + _dot(p_ref[g], v_ext)

    def advance(j, prev, prev_bufs, next_bufs):
        s_all = scores(j)
        values(prev, *prev_bufs)
        softmax(s_all, *next_bufs, masked=False)

    buf_a = (alpha_a, p_a)
    buf_b = (alpha_b, p_b)
    softmax(scores(qi), *buf_a, masked=True)

    def pair(i, prev):
        advance(2 * i, prev, buf_a, buf_b)
        advance(2 * i + 1, 2 * i, buf_b, buf_a)
        return 2 * i + 1

    prev = lax.fori_loop(0, qi // 2, pair, qi)

    @pl.when(qi % 2 == 1)
    def _():
        advance(qi - 1, prev, buf_a, buf_b)
        values(qi - 1, *buf_b)

    @pl.when(qi % 2 == 0)
    def _():
        values(prev, *buf_a)

    for g in range(G):
        acc = acc_ref[g]
        o_ref[0, :, g * MLA_V:(g + 1) * MLA_V] = acc[:, :MLA_V] / acc[:, MLA_V:]


def _attention(q, k, v):
    B, H, S, _ = q.shape
    T = ATTN_TILE
    G = ATTN_HEADS_PER_STEP
    return pl.pallas_call(
        _attn_kernel,
        grid=(B, H // G, S // T),
        in_specs=[
            pl.BlockSpec((1, G, T, HEAD_PAD), lambda b, h, i: (b, h, i, 0)),
            pl.BlockSpec((1, G, S, HEAD_PAD), lambda b, h, i: (b, h, 0, 0)),
            pl.BlockSpec((1, G, S, MLA_V), lambda b, h, i: (b, h, 0, 0)),
        ],
        out_specs=pl.BlockSpec((1, T, G * MLA_V), lambda b, h, i: (b, i, h)),
        out_shape=jax.ShapeDtypeStruct((B, S, H * MLA_V), F32),
        scratch_shapes=[pltpu.VMEM((G, T, LANES), F32), pltpu.VMEM((G, T, 2 * MLA_V), F32),
                        pltpu.VMEM((G, T, LANES), F32), pltpu.VMEM((G, T, T), BF16),
                        pltpu.VMEM((G, T, LANES), F32), pltpu.VMEM((G, T, T), BF16)],
        compiler_params=pltpu.CompilerParams(
            dimension_semantics=("parallel", "parallel", "parallel"), vmem_limit_bytes=VMEM_LIMIT),
        name="attn",
    )(q, k, v)


def _cmul(ar, ai, br, bi):
    return ar * br - ai * bi, ar * bi + ai * br


def _s5prep_kernel(lr_ref, li_ref, ldt_ref, btr_ref, bti_ref, par_ref, pai_ref, tr_ref, ti_ref, bbr_ref, bbi_ref):
    lr = lr_ref[...]
    li = li_ref[...]
    dt = jnp.exp(ldt_ref[...])
    decay = jnp.exp(lr * dt)
    ar = decay * jnp.cos(li * dt)
    ai = decay * jnp.sin(li * dt)
    den = lr * lr + li * li
    nr = ar - 1.0
    coef_r = (nr * lr + ai * li) / den
    coef_i = (ai * lr - nr * li) / den
    br = btr_ref[...]
    bi = bti_ref[...]
    bbr_ref[...] = coef_r * br - coef_i * bi
    bbi_ref[...] = coef_r * bi + coef_i * br
    pr, pi = ar, ai
    for k in range(SSM_SUB):
        if k:
            pr, pi = _cmul(pr, pi, ar, ai)
        par_ref[k * SUBLANES:(k + 1) * SUBLANES, :] = jnp.broadcast_to(pr, (SUBLANES, SSM_STATES))
        pai_ref[k * SUBLANES:(k + 1) * SUBLANES, :] = jnp.broadcast_to(pi, (SUBLANES, SSM_STATES))
    a32r, a32i = pr, pi
    qr, qi = jnp.ones_like(ar), jnp.zeros_like(ar)
    for r in range(SUBLANES + 1):
        tr_ref[r:r + 1, :] = qr
        ti_ref[r:r + 1, :] = qi
        qr, qi = _cmul(qr, qi, a32r, a32i)
    a64r, a64i = _cmul(a32r, a32i, a32r, a32i)
    a128r, a128i = _cmul(a64r, a64i, a64r, a64i)
    for row, (vr, vi) in ((ROW_A32, (a32r, a32i)), (ROW_A64, (a64r, a64i)), (ROW_A128, (a128r, a128i))):
        tr_ref[row:row + 1, :] = vr
        ti_ref[row:row + 1, :] = vi
    zeros = jnp.zeros((TABLE_ROWS - ROW_A128 - 1, SSM_STATES), F32)
    tr_ref[ROW_A128 + 1:, :] = zeros
    ti_ref[ROW_A128 + 1:, :] = zeros


def _s5prep(lr, li, ldt, btr, bti):
    pa = jax.ShapeDtypeStruct((SSM_TILE, SSM_STATES), F32)
    tab = jax.ShapeDtypeStruct((TABLE_ROWS, SSM_STATES), F32)
    bb = jax.ShapeDtypeStruct((SSM_GROUP, SSM_STATES), F32)
    return pl.pallas_call(_s5prep_kernel, out_shape=[pa, pa, tab, tab, bb, bb], name="s5prep")(
        lr, li, ldt, btr, bti)


def _s5_kernel(u_ref, perm_ref, permt_ref, bblk_ref, cblk_ref, par_ref, pai_ref, tr_ref, ti_ref, d_ref,
               y_ref, st_ref, xb_ref, hr_ref, hi_ref):
    N = SSM_STATES
    H = SSM_HALF
    W = SSM_WIDTH // SSM_BLOCKS

    @pl.when(pl.program_id(1) == 0)
    def _():
        hr_ref[...] = jnp.zeros_like(hr_ref)
        hi_ref[...] = jnp.zeros_like(hi_ref)

    u = u_ref[0]
    u_perm = _dot(perm_ref[...], u.astype(BF16)).astype(BF16)
    for blk in range(SSM_BLOCKS):
        bu = _dot(u_perm[:, blk * W:(blk + 1) * W], bblk_ref[blk])
        st_ref[:, blk * H:(blk + 1) * H] = bu[:, :H]
        st_ref[:, N + blk * H:N + (blk + 1) * H] = bu[:, H:]

    def local_step(k, carry):
        xr, xi = carry
        rows = pl.ds(pl.multiple_of(k * SUBLANES, SUBLANES), SUBLANES)
        ar = par_ref[0:SUBLANES, :]
        ai = pai_ref[0:SUBLANES, :]
        nr = ar * xr - ai * xi + st_ref[rows, :N]
        ni = ar * xi + ai * xr + st_ref[rows, N:]
        st_ref[rows, :N] = nr
        st_ref[rows, N:] = ni
        return nr, ni

    zero = jnp.zeros((SUBLANES, N), F32)
    fr, fi = lax.fori_loop(0, SSM_SUB, local_step, (zero, zero), unroll=2)

    ridx = lax.broadcasted_iota(jnp.int32, (SUBLANES, N), 0)

    def shifted(x, s):
        return jnp.where(ridx >= s, pltpu.roll(x, s, 0), 0.0)

    for s, row in ((1, ROW_A32), (2, ROW_A64), (4, ROW_A128)):
        mr, mi = _cmul(tr_ref[row:row + 1, :], ti_ref[row:row + 1, :], shifted(fr, s), shifted(fi, s))
        fr, fi = fr + mr, fi + mi
    hr = hr_ref[...]
    hi = hi_ref[...]
    cr, ci = _cmul(tr_ref[0:ROW_Q8, :], ti_ref[0:ROW_Q8, :], hr, hi)
    cr = cr + shifted(fr, 1)
    ci = ci + shifted(fi, 1)
    nhr, nhi = _cmul(tr_ref[ROW_Q8:ROW_Q8 + 1, :], ti_ref[ROW_Q8:ROW_Q8 + 1, :], hr, hi)
    hr_ref[...] = nhr + fr[SUBLANES - 1:, :]
    hi_ref[...] = nhi + fi[SUBLANES - 1:, :]

    def fix_step(kk, _):
        xr, xi = [], []
        for half in range(2):
            rows = pl.ds(pl.multiple_of((2 * kk + half) * SUBLANES, SUBLANES), SUBLANES)
            pr = par_ref[rows, :]
            pi = pai_ref[rows, :]
            xr.append(st_ref[rows, :N] + (pr * cr - pi * ci))
            xi.append(st_ref[rows, N:] + (pr * ci + pi * cr))
        rows16 = pl.ds(pl.multiple_of(kk * 2 * SUBLANES, 2 * SUBLANES), 2 * SUBLANES)
        xb_ref[rows16, :N] = jnp.concatenate(xr, axis=0).astype(BF16)
        xb_ref[rows16, N:] = jnp.concatenate(xi, axis=0).astype(BF16)
        return 0

    lax.fori_loop(0, SSM_SUB // 2, fix_step, 0)

    y_blocks = []
    for blk in range(SSM_BLOCKS):
        x_blk = jnp.concatenate([xb_ref[:, blk * H:(blk + 1) * H], xb_ref[:, N + blk * H:N + (blk + 1) * H]],
                                axis=-1)
        y_blocks.append(_dot(x_blk, cblk_ref[blk]))
    y_perm = jnp.concatenate(y_blocks, axis=-1)
    y_hi = y_perm.astype(BF16)
    y_lo = (y_perm - y_hi.astype(F32)).astype(BF16)
    y_ref[0] = _dot(permt_ref[...], y_hi) + _dot(permt_ref[...], y_lo) + d_ref[...] * u


def _s5(u, perm, permt, bblk, cblk, pa_r, pa_i, tab_r, tab_i, d_row):
    B, S, W = u.shape
    T = SSM_TILE
    consts = [perm, permt, bblk, cblk, pa_r, pa_i, tab_r, tab_i, d_row]
    return pl.pallas_call(
        _s5_kernel,
        grid=(B, S // T),
        in_specs=[pl.BlockSpec((1, T, W), lambda b, i: (b, i, 0))] + [_const_spec(c.shape) for c in consts],
        out_specs=pl.BlockSpec((1, T, W), lambda b, i: (b, i, 0)),
        out_shape=jax.ShapeDtypeStruct((B, S, W), F32),
        scratch_shapes=[
            pltpu.VMEM((T, 2 * SSM_STATES), F32),
            pltpu.VMEM((T, 2 * SSM_STATES), BF16),
            pltpu.VMEM((1, SSM_STATES), F32),
            pltpu.VMEM((1, SSM_STATES), F32),
        ],
        compiler_params=pltpu.CompilerParams(
            dimension_semantics=("parallel", "arbitrary"), vmem_limit_bytes=VMEM_LIMIT),
        name="s5",
    )(u, *consts)


def _memkv_kernel(mem_ref, gmem_ref, wkv_ref, gk_ref, kt_ref, v_ref):
    m = _rms(mem_ref[0], gmem_ref[...]).astype(BF16)
    kv = _dot(m, wkv_ref[...])
    gk = gk_ref[...]
    for hh in range(XATTN_HEADS):
        kh = kv[:, hh * XATTN_HEAD_DIM:(hh + 1) * XATTN_HEAD_DIM]
        kt_ref[0, hh] = _rms(kh, gk).T.astype(BF16)
    v_ref[0] = kv[:, XATTN_WIDTH:].astype(BF16)


def _memkv(mem, gmem, wkv, gk):
    B, M, D = mem.shape
    return pl.pallas_call(
        _memkv_kernel,
        grid=(B,),
        in_specs=[pl.BlockSpec((1, M, D), lambda b: (b, 0, 0)), _const_spec(gmem.shape),
                  _const_spec(wkv.shape), _const_spec(gk.shape)],
        out_specs=[pl.BlockSpec((1, XATTN_HEADS, XATTN_HEAD_DIM, M), lambda b: (b, 0, 0, 0)),
                   pl.BlockSpec((1, M, XATTN_WIDTH), lambda b: (b, 0, 0))],
        out_shape=[jax.ShapeDtypeStruct((B, XATTN_HEADS, XATTN_HEAD_DIM, M), BF16),
                   jax.ShapeDtypeStruct((B, M, XATTN_WIDTH), BF16)],
        compiler_params=pltpu.CompilerParams(dimension_semantics=("parallel",)),
        name="memkv",
    )(mem, gmem, wkv, gk)


def _back_kernel(x1_ref, omla_ref, yssm_ref, kt_ref, vm_ref, wglu_ref, bglu_ref, gossm_ref, gomla_ref, wo_ref,
                 gx_ref, wq_ref, gqn_ref, wxo_ref, g2_ref, wg_ref, wu_ref, wd_ref, out_ref):
    g = jax.nn.gelu(yssm_ref[0])
    glu = g * jax.nn.sigmoid(_dot(g.astype(BF16), wglu_ref[...]) + bglu_ref[...])
    y_ssm = _rms(glu, gossm_ref[...]).astype(BF16)
    y_mla = _rms(omla_ref[0], gomla_ref[...]).astype(BF16)
    y = jnp.concatenate([y_mla, y_ssm], axis=-1)
    x2 = x1_ref[0] + _dot(y, wo_ref[...])
    hq = _rms(x2, gx_ref[...]).astype(BF16)
    q = _dot(hq, wq_ref[...])
    scale = XATTN_HEAD_DIM ** -0.5
    heads = []
    for hh in range(XATTN_HEADS):
        lo = hh * XATTN_HEAD_DIM
        qh = (_rms(q[:, lo:lo + XATTN_HEAD_DIM], gqn_ref[...]) * scale).astype(BF16)
        s = _dot(qh, kt_ref[0, hh])
        p = jnp.exp(s - jnp.max(s, axis=-1, keepdims=True))
        p = p / jnp.sum(p, axis=-1, keepdims=True)
        heads.append(_dot(p.astype(BF16), vm_ref[0, :, lo:lo + XATTN_HEAD_DIM]))
    o = jnp.concatenate(heads, axis=-1).astype(BF16)
    x3 = x2 + _dot(o, wxo_ref[...])
    out_ref[0] = _ffn_half_step(x3, g2_ref, wg_ref, wu_ref, wd_ref)


def _back(x1, omla, yssm, kt, vm, wglu, bglu, gossm, gomla, wo, gx, wq, gqn, wxo, g2, wg, wu, wd):
    B, S, D = x1.shape
    T = TOKEN_TILE
    M = vm.shape[1]
    row = lambda w: pl.BlockSpec((1, T, w), lambda b, i: (b, i, 0))
    consts = [wglu, bglu, gossm, gomla, wo, gx, wq, gqn, wxo, g2, wg, wu, wd]
    return pl.pallas_call(
        _back_kernel,
        grid=(B, S // T),
        in_specs=[row(D), row(MLA_WIDTH), row(SSM_WIDTH),
                  pl.BlockSpec((1, XATTN_HEADS, XATTN_HEAD_DIM, M), lambda b, i: (b, 0, 0, 0)),
                  pl.BlockSpec((1, M, XATTN_WIDTH), lambda b, i: (b, 0, 0))]
                 + [_const_spec(c.shape) for c in consts],
        out_specs=row(D),
        out_shape=jax.ShapeDtypeStruct((B, S, D), F32),
        compiler_params=pltpu.CompilerParams(
            dimension_semantics=("parallel", "parallel"), vmem_limit_bytes=VMEM_LIMIT),
        name="back",
    )(x1, omla, yssm, kt, vm, *consts)


def _row(v):
    return v.reshape(1, -1).astype(F32)


def _ffn_weights(w_gate, w_up, w_down):
    pad = D_FF_PAD - D_FF
    return (jnp.pad(w_gate, ((0, 0), (0, pad))).astype(BF16),
            jnp.pad(w_up, ((0, 0), (0, pad))).astype(BF16),
            jnp.pad(w_down, ((0, pad), (0, 0))).astype(BF16))


def _pad_rope_cols(w):
    half = MLA_ROPE // 2
    z = jnp.zeros(w.shape[:-1] + (half,), w.dtype)
    return jnp.concatenate([w[..., :half], z, w[..., half:], z], axis=-1)


def _pad_head_cols(w):
    return jnp.concatenate([w[..., :MLA_NOPE], _pad_rope_cols(w[..., MLA_NOPE:])], axis=-1)


def _block_diag(blocks):
    G, r, c = blocks.shape
    eye = jnp.eye(G, dtype=blocks.dtype)
    return jnp.einsum('grc,gk->grkc', blocks, eye).reshape(G * r, G * c)


def kernel(x, mem, positions, ffn1_norm, ffn1_w_gate, ffn1_w_up, ffn1_w_down, mix_norm, w_in, mla_q_norm, mla_w_uq, mla_kv_norm, mla_w_ukv, mla_qk_norm_q, mla_qk_norm_k, ssm_a_re, ssm_a_im, ssm_log_dt, ssm_b_re, ssm_b_im, ssm_c_re, ssm_c_im, ssm_d, ssm_w_glu, ssm_b_glu, out_norm_mla, out_norm_ssm, w_o, xattn_norm, mem_norm, xattn_w_q, xattn_w_kv, xattn_q_norm, xattn_k_norm, xattn_w_o, ffn2_norm, ffn2_w_gate, ffn2_w_up, ffn2_w_down):
    depth = ffn1_norm.shape[0]
    B, S, _ = x.shape
    pos3 = positions.reshape(B, S, 1)
    half = MLA_ROPE // 2
    inv = ROPE_THETA ** (-jnp.arange(half, dtype=F32) / half)
    zeros = jnp.zeros((half,), F32)
    ones = jnp.ones((half,), F32)
    inv_tab = jnp.concatenate([inv, zeros, inv, zeros]).reshape(1, ROPE_PAD)
    sgn_tab = jnp.concatenate([-ones, zeros, ones, zeros]).reshape(1, ROPE_PAD)
    j = jnp.arange(SSM_TILE)
    src = (j % SUBLANES) * SSM_SUB + j // SUBLANES
    perm = (src[:, None] == j[None, :]).astype(BF16)
    permt = perm.T

    for l in range(depth):
        f1 = _ffn_weights(ffn1_w_gate[l], ffn1_w_up[l], ffn1_w_down[l])
        f2 = _ffn_weights(ffn2_w_gate[l], ffn2_w_up[l], ffn2_w_down[l])
        o_kr = MLA_Q_RANK + MLA_KV_RANK
        win = jnp.concatenate([w_in[l][:, :o_kr], _pad_rope_cols(w_in[l][:, o_kr:o_kr + MLA_ROPE]),
                               w_in[l][:, o_kr + MLA_ROPE:]], axis=-1).astype(BF16)
        wuq = _pad_head_cols(mla_w_uq[l].reshape(MLA_Q_RANK, MLA_HEADS, MLA_QK)).reshape(
            MLA_Q_RANK, MLA_HEADS * HEAD_PAD).astype(BF16)
        gqk_q = _row(_pad_head_cols(mla_qk_norm_q[l]))
        gqk_k = _row(_pad_head_cols(mla_qk_norm_k[l]))

        x1, q, k, v, u = _front(
            x, pos3, _row(ffn1_norm[l]), *f1, _row(mix_norm[l]), win, _row(mla_q_norm[l]), wuq,
            _row(mla_kv_norm[l]), mla_w_ukv[l].astype(BF16), gqk_q, gqk_k, inv_tab, sgn_tab)

        o_mla = _attention(q, k, v)

        rows = lambda a: a.reshape(1, SSM_STATES).astype(F32)
        ldt = jnp.repeat(ssm_log_dt[l].astype(F32), SSM_STATE).reshape(1, SSM_STATES)
        bt = lambda b: b.astype(F32).transpose(2, 0, 1).reshape(SSM_GROUP, SSM_STATES)
        pa_r, pa_i, tab_r, tab_i, bb_r, bb_i = _s5prep(
            rows(ssm_a_re[l]), rows(ssm_a_im[l]), ldt, bt(ssm_b_re[l]), bt(ssm_b_im[l]))
        gpb = SSM_GROUPS // SSM_BLOCKS
        b_blocks = lambda m: m.reshape(SSM_GROUP, SSM_GROUPS, SSM_STATE).transpose(1, 0, 2)
        c_blocks = lambda c: c.astype(F32).transpose(0, 2, 1)
        bblk = jnp.stack([
            jnp.concatenate([_block_diag(b_blocks(bb_r)[i * gpb:(i + 1) * gpb]),
                             _block_diag(b_blocks(bb_i)[i * gpb:(i + 1) * gpb])], axis=-1)
            for i in range(SSM_BLOCKS)]).astype(BF16)
        cblk = jnp.stack([
            jnp.concatenate([_block_diag(c_blocks(ssm_c_re[l])[i * gpb:(i + 1) * gpb]),
                             -_block_diag(c_blocks(ssm_c_im[l])[i * gpb:(i + 1) * gpb])], axis=0)
            for i in range(SSM_BLOCKS)]).astype(BF16)
        y_ssm = _s5(u, perm, permt, bblk, cblk, pa_r, pa_i, tab_r, tab_i, _row(ssm_d[l]))

        kt, vm = _memkv(mem, _row(mem_norm[l]), xattn_w_kv[l].astype(BF16), _row(xattn_k_norm[l]))

        x = _back(x1, o_mla, y_ssm, kt, vm, ssm_w_glu[l].astype(BF16), _row(ssm_b_glu[l]), _row(out_norm_ssm[l]),
                  _row(out_norm_mla[l]), w_o[l].astype(BF16), _row(xattn_norm[l]),
                  xattn_w_q[l].astype(BF16), _row(xattn_q_norm[l]), xattn_w_o[l].astype(BF16),
                  _row(ffn2_norm[l]), *f2)
    return x
```

```python
import functools
import math

import jax
import jax.numpy as jnp
from jax import lax
from jax.experimental import pallas as pl
from jax.experimental.pallas import tpu as pltpu

F32 = jnp.float32
BF16 = jnp.bfloat16

D_MODEL = 1024
D_FF = 2752
MLA_HEADS = 4
MLA_Q_RANK = 384
MLA_KV_RANK = 256
MLA_NOPE = 128
MLA_ROPE = 64
MLA_V = 128
MLA_QK = MLA_NOPE + MLA_ROPE
MLA_WIDTH = MLA_HEADS * MLA_V
SSM_WIDTH = D_MODEL - MLA_WIDTH
SSM_GROUP = 16
SSM_GROUPS = SSM_WIDTH // SSM_GROUP
SSM_STATE = 64
XATTN_HEADS = 4
XATTN_HEAD_DIM = 128
XATTN_WIDTH = XATTN_HEADS * XATTN_HEAD_DIM
ROPE_THETA = 10000.0
EPS = 1e-6

LANES = 128
SUBLANES = 8
D_FF_PAD = 2816
FF_CHUNK = D_FF_PAD // 2
HEAD_PAD = 2 * LANES
ROPE_PAD = LANES
IN_PAD = MLA_Q_RANK + MLA_KV_RANK + ROPE_PAD + SSM_WIDTH
TOKEN_TILE = 512
ATTN_TILE = 512
ATTN_HEADS_PER_STEP = 4
LOG2_E = math.log2(math.e)
SSM_STATES = SSM_GROUPS * SSM_STATE
SSM_TILE = 256
SSM_SUB = SSM_TILE // SUBLANES
SSM_BLOCKS = 2
SSM_HALF = SSM_STATES // SSM_BLOCKS
TABLE_ROWS = 16
ROW_Q8 = SUBLANES
ROW_A32 = ROW_Q8 + 1
ROW_A64 = ROW_Q8 + 2
ROW_A128 = ROW_Q8 + 3
VMEM_LIMIT = 56 * 1024 * 1024


def _rms(x, g):
    return x * lax.rsqrt(jnp.mean(x * x, axis=-1, keepdims=True) + EPS) * g


def _dot(a, b):
    return jnp.dot(a, b, preferred_element_type=F32)


def _dot_nt(a, b_t):
    return lax.dot_general(a, b_t, (((1,), (1,)), ((), ())), preferred_element_type=F32)


def _ffn_half_step(x, g_ref, wgt_ref, wut_ref, wd_ref):
    h = _rms(x, g_ref[...]).astype(BF16)
    acc = None
    for c in range(D_FF_PAD // FF_CHUNK):
        lo = c * FF_CHUNK
        gate = _dot_nt(h, wgt_ref[lo:lo + FF_CHUNK, :])
        up = _dot_nt(h, wut_ref[lo:lo + FF_CHUNK, :])
        act = (gate * jax.nn.sigmoid(gate) * up).astype(BF16)
        part = _dot(act, wd_ref[lo:lo + FF_CHUNK, :])
        acc = part if acc is None else acc + part
    return x + 0.5 * acc


def _rope(x, cos, sin_signed):
    return x * cos + pltpu.roll(x, ROPE_PAD // 2, 1) * sin_signed


def _front_kernel(x_ref, pos_ref, g1_ref, wgt_ref, wut_ref, wd_ref, gmix_ref, wint_ref,
                  gq_ref, wuq_ref, gkv_ref, wukv_ref, gqk_q_ref, gqk_k_ref, inv_ref, sgn_ref,
                  x1_ref, q_ref, k_ref, v_ref, u_ref, qraw_sc, kv_sc, kr_sc):
    @pl.when(pl.program_id(0) == 0)
    def _():
        qraw_sc[...] = jnp.zeros_like(qraw_sc)
        kv_sc[...] = jnp.zeros_like(kv_sc)
        kr_sc[...] = jnp.zeros_like(kr_sc)

    ang = pos_ref[0].astype(F32) * inv_ref[...]
    cos = jnp.cos(ang)
    sin = jnp.sin(ang) * sgn_ref[...]
    gq = gqk_q_ref[...]
    gk = gqk_k_ref[...]
    kr = kr_sc[...]
    kr_ss = jnp.sum(kr * kr, axis=-1, keepdims=True)
    kr_rot = _rope(kr * gk[:, MLA_NOPE:], cos, sin)
    inv_qk = 1.0 / MLA_QK
    scale = MLA_QK ** -0.5 * LOG2_E
    for hh in range(MLA_HEADS):
        qh = qraw_sc[:, hh * HEAD_PAD:(hh + 1) * HEAD_PAD]
        qn = qh * lax.rsqrt(jnp.sum(qh * qh, axis=-1, keepdims=True) * inv_qk + EPS) * gq
        q_rot = _rope(qn[:, MLA_NOPE:], cos, sin)
        q_ref[0, hh] = (jnp.concatenate([qn[:, :MLA_NOPE], q_rot], axis=-1) * scale).astype(BF16)
        kn = kv_sc[:, hh * HEAD_PAD:hh * HEAD_PAD + MLA_NOPE]
        inv_k = lax.rsqrt((jnp.sum(kn * kn, axis=-1, keepdims=True) + kr_ss) * inv_qk + EPS)
        k_ref[0, hh] = jnp.concatenate([kn * inv_k * gk[:, :MLA_NOPE], kr_rot * inv_k], axis=-1).astype(BF16)
        v_ref[0, hh] = kv_sc[:, hh * HEAD_PAD + MLA_NOPE:(hh + 1) * HEAD_PAD].astype(BF16)

    x1 = _ffn_half_step(x_ref[0], g1_ref, wgt_ref, wut_ref, wd_ref)
    x1_ref[0] = x1
    h = _rms(x1, gmix_ref[...]).astype(BF16)
    proj = _dot_nt(h, wint_ref[...])
    o_ckv = MLA_Q_RANK
    o_kr = o_ckv + MLA_KV_RANK
    o_u = o_kr + ROPE_PAD
    u_ref[0] = proj[:, o_u:]
    cq = _rms(proj[:, :o_ckv], gq_ref[...]).astype(BF16)
    ckv = _rms(proj[:, o_ckv:o_kr], gkv_ref[...]).astype(BF16)
    qraw_sc[...] = _dot(cq, wuq_ref[...])
    kv_sc[...] = _dot(ckv, wukv_ref[...])
    kr_sc[...] = proj[:, o_kr:o_u]


def _const_spec(shape):
    return pl.BlockSpec(shape, lambda *_: (0,) * len(shape), pipeline_mode=pl.Buffered(1))


def _front(x, pos3, g1, wgt, wut, wd, gmix, wint, gq, wuq, gkv, wukv, gqk_q, gqk_k, inv_tab, sgn_tab):
    B, S, D = x.shape
    T = TOKEN_TILE
    per_seq = S // T
    n_tiles = B * per_seq

    def cur(s):
        t = jnp.minimum(s, n_tiles - 1)
        return t // per_seq, t % per_seq

    def prev(s):
        t = jnp.maximum(s - 1, 0)
        return t // per_seq, t % per_seq

    row = lambda w, tile: pl.BlockSpec((1, T, w), lambda s: (*tile(s), 0))
    head = lambda w: pl.BlockSpec((1, MLA_HEADS, T, w), lambda s: (prev(s)[0], 0, prev(s)[1], 0))
    consts = [g1, wgt, wut, wd, gmix, wint, gq, wuq, gkv, wukv, gqk_q, gqk_k, inv_tab, sgn_tab]
    return pl.pallas_call(
        _front_kernel,
        grid=(n_tiles + 1,),
        in_specs=[row(D, cur), row(1, prev)] + [_const_spec(c.shape) for c in consts],
        out_specs=[row(D, cur), head(HEAD_PAD), head(HEAD_PAD), head(MLA_V), row(SSM_WIDTH, cur)],
        out_shape=[
            jax.ShapeDtypeStruct((B, S, D), F32),
            jax.ShapeDtypeStruct((B, MLA_HEADS, S, HEAD_PAD), BF16),
            jax.ShapeDtypeStruct((B, MLA_HEADS, S, HEAD_PAD), BF16),
            jax.ShapeDtypeStruct((B, MLA_HEADS, S, MLA_V), BF16),
            jax.ShapeDtypeStruct((B, S, SSM_WIDTH), F32),
        ],
        scratch_shapes=[pltpu.VMEM((T, MLA_HEADS * HEAD_PAD), F32), pltpu.VMEM((T, MLA_HEADS * HEAD_PAD), F32),
                        pltpu.VMEM((T, ROPE_PAD), F32)],
        compiler_params=pltpu.CompilerParams(dimension_semantics=("arbitrary",), vmem_limit_bytes=VMEM_LIMIT),
        name="front",
    )(x, pos3, *consts)


def _attn_kernel(q_ref, k_ref, v_ref, o_ref, m_ref, acc_ref, alpha_a, p_a, alpha_b, p_b):
    T = ATTN_TILE
    G = ATTN_HEADS_PER_STEP
    qi = pl.program_id(2)
    m_ref[...] = jnp.full(m_ref.shape, -jnp.inf, F32)
    acc_ref[...] = jnp.zeros(acc_ref.shape, F32)
    ones = jnp.ones((T, LANES), BF16)

    def tile_rows(j):
        return pl.ds(pl.multiple_of(j * T, T), T)

    def scores(j):
        return [lax.dot_general(q_ref[0, g], k_ref[0, g, tile_rows(j), :], (((1,), (1,)), ((), ())),
                                preferred_element_type=F32) for g in range(G)]

    def softmax(s_all, alpha_ref, p_ref, masked):
        for g, s in enumerate(s_all):
            if masked:
                rows = lax.broadcasted_iota(jnp.int32, (T, T), 0)
                cols = lax.broadcasted_iota(jnp.int32, (T, T), 1)
                s = jnp.where(rows >= cols, s, -jnp.inf)
            m_prev = m_ref[g]
            m_new = jnp.maximum(m_prev, jnp.max(s, axis=-1, keepdims=True))
            alpha_ref[g] = jnp.exp2(m_prev - m_new)
            p_ref[g] = jnp.exp2(s - jnp.tile(m_new, (1, T // LANES))).astype(BF16)
            m_ref[g] = m_new

    def values(j, alpha_ref, p_ref):
        for g in range(G):
            v_ext = jnp.concatenate([v_ref[0, g, tile_rows(j), :], ones], axis=-1)
            acc_ref[g] = jnp.tile(alpha_ref[g], (1, 2)) * acc_ref[g] + _dot(p_ref[g], v_ext)

    def advance(j, prev, prev_bufs, next_bufs):
        s_all = scores(j)
        values(prev, *prev_bufs)
        softmax(s_all, *next_bufs, masked=False)

    buf_a = (alpha_a, p_a)
    buf_b = (alpha_b, p_b)
    softmax(scores(qi), *buf_a, masked=True)

    def pair(i, prev):
        advance(2 * i, prev, buf_a, buf_b)
        advance(2 * i + 1, 2 * i, buf_b, buf_a)
        return 2 * i + 1

    prev = lax.fori_loop(0, qi // 2, pair, qi)

    @pl.when(qi % 2 == 1)
    def _():
        advance(qi - 1, prev, buf_a, buf_b)
        values(qi - 1, *buf_b)

    @pl.when(qi % 2 == 0)
    def _():
        values(prev, *buf_a)

    for g in range(G):
        acc = acc_ref[g]
        o_ref[0, :, g * MLA_V:(g + 1) * MLA_V] = acc[:, :MLA_V] / acc[:, MLA_V:]


def _attention(q, k, v):
    B, H, S, _ = q.shape
    T = ATTN_TILE
    G = ATTN_HEADS_PER_STEP
    return pl.pallas_call(
        _attn_kernel,
        grid=(B, H // G, S // T),
        in_specs=[
            pl.BlockSpec((1, G, T, HEAD_PAD), lambda b, h, i: (b, h, i, 0)),
            pl.BlockSpec((1, G, S, HEAD_PAD), lambda b, h, i: (b, h, 0, 0)),
            pl.BlockSpec((1, G, S, MLA_V), lambda b, h, i: (b, h, 0, 0)),
        ],
        out_specs=pl.BlockSpec((1, T, G * MLA_V), lambda b, h, i: (b, i, h)),
        out_shape=jax.ShapeDtypeStruct((B, S, H * MLA_V), F32),
        scratch_shapes=[pltpu.VMEM((G, T, LANES), F32), pltpu.VMEM((G, T, 2 * MLA_V), F32),
                        pltpu.VMEM((G, T, LANES), F32), pltpu.VMEM((G, T, T), BF16),
                        pltpu.VMEM((G, T, LANES), F32), pltpu.VMEM((G, T, T), BF16)],
        compiler_params=pltpu.CompilerParams(
            dimension_semantics=("parallel", "parallel", "parallel"), vmem_limit_bytes=VMEM_LIMIT),
        name="attn",
    )(q, k, v)


def _cmul(ar, ai, br, bi):
    return ar * br - ai * bi, ar * bi + ai * br


def _s5prep_kernel(lr_ref, li_ref, ldt_ref, btr_ref, bti_ref, par_ref, pai_ref, tr_ref, ti_ref, bbr_ref, bbi_ref):
    lr = lr_ref[...]
    li = li_ref[...]
    dt = jnp.exp(ldt_ref[...])
    decay = jnp.exp(lr * dt)
    ar = decay * jnp.cos(li * dt)
    ai = decay * jnp.sin(li * dt)
    den = lr * lr + li * li
    nr = ar - 1.0
    coef_r = (nr * lr + ai * li) / den
    coef_i = (ai * lr - nr * li) / den
    br = btr_ref[...]
    bi = bti_ref[...]
    bbr_ref[...] = coef_r * br - coef_i * bi
    bbi_ref[...] = coef_r * bi + coef_i * br
    pr, pi = ar, ai
    for k in range(SSM_SUB):
        if k:
            pr, pi = _cmul(pr, pi, ar, ai)
        par_ref[k * SUBLANES:(k + 1) * SUBLANES, :] = jnp.broadcast_to(pr, (SUBLANES, SSM_STATES))
        pai_ref[k * SUBLANES:(k + 1) * SUBLANES, :] = jnp.broadcast_to(pi, (SUBLANES, SSM_STATES))
    a32r, a32i = pr, pi
    qr, qi = jnp.ones_like(ar), jnp.zeros_like(ar)
    for r in range(SUBLANES + 1):
        tr_ref[r:r + 1, :] = qr
        ti_ref[r:r + 1, :] = qi
        qr, qi = _cmul(qr, qi, a32r, a32i)
    a64r, a64i = _cmul(a32r, a32i, a32r, a32i)
    a128r, a128i = _cmul(a64r, a64i, a64r, a64i)
    for row, (vr, vi) in ((ROW_A32, (a32r, a32i)), (ROW_A64, (a64r, a64i)), (ROW_A128, (a128r, a128i))):
        tr_ref[row:row + 1, :] = vr
        ti_ref[row:row + 1, :] = vi
    zeros = jnp.zeros((TABLE_ROWS - ROW_A128 - 1, SSM_STATES), F32)
    tr_ref[ROW_A128 + 1:, :] = zeros
    ti_ref[ROW_A128 + 1:, :] = zeros


def _s5prep(lr, li, ldt, btr, bti):
    pa = jax.ShapeDtypeStruct((SSM_TILE, SSM_STATES), F32)
    tab = jax.ShapeDtypeStruct((TABLE_ROWS, SSM_STATES), F32)
    bb = jax.ShapeDtypeStruct((SSM_GROUP, SSM_STATES), F32)
    return pl.pallas_call(_s5prep_kernel, out_shape=[pa, pa, tab, tab, bb, bb], name="s5prep")(
        lr, li, ldt, btr, bti)


def _s5_kernel(u_ref, perm_ref, permt_ref, bblk_ref, cblk_ref, par_ref, pai_ref, tr_ref, ti_ref, d_ref,
               y_ref, st_ref, xb_ref, hr_ref, hi_ref):
    N = SSM_STATES
    H = SSM_HALF
    W = SSM_WIDTH // SSM_BLOCKS

    @pl.when(pl.program_id(1) == 0)
    def _():
        hr_ref[...] = jnp.zeros_like(hr_ref)
        hi_ref[...] = jnp.zeros_like(hi_ref)

    u = u_ref[0]
    u_perm = _dot(perm_ref[...], u.astype(BF16)).astype(BF16)
    for blk in range(SSM_BLOCKS):
        bu = _dot(u_perm[:, blk * W:(blk + 1) * W], bblk_ref[blk])
        st_ref[:, blk * H:(blk + 1) * H] = bu[:, :H]
        st_ref[:, N + blk * H:N + (blk + 1) * H] = bu[:, H:]

    def local_step(k, carry):
        xr, xi = carry
        rows = pl.ds(pl.multiple_of(k * SUBLANES, SUBLANES), SUBLANES)
        ar = par_ref[0:SUBLANES, :]
        ai = pai_ref[0:SUBLANES, :]
        nr = ar * xr - ai * xi + st_ref[rows, :N]
        ni = ar * xi + ai * xr + st_ref[rows, N:]
        st_ref[rows, :N] = nr
        st_ref[rows, N:] = ni
        return nr, ni

    zero = jnp.zeros((SUBLANES, N), F32)
    fr, fi = lax.fori_loop(0, SSM_SUB, local_step, (zero, zero), unroll=2)

    ridx = lax.broadcasted_iota(jnp.int32, (SUBLANES, N), 0)

    def shifted(x, s):
        return jnp.where(ridx >= s, pltpu.roll(x, s, 0), 0.0)

    for s, row in ((1, ROW_A32), (2, ROW_A64), (4, ROW_A128)):
        mr, mi = _cmul(tr_ref[row:row + 1, :], ti_ref[row:row + 1, :], shifted(fr, s), shifted(fi, s))
        fr, fi = fr + mr, fi + mi
    hr = hr_ref[...]
    hi = hi_ref[...]
    cr, ci = _cmul(tr_ref[0:ROW_Q8, :], ti_ref[0:ROW_Q8, :], hr, hi)
    cr = cr + shifted(fr, 1)
    ci = ci + shifted(fi, 1)
    nhr, nhi = _cmul(tr_ref[ROW_Q8:ROW_Q8 + 1, :], ti_ref[ROW_Q8:ROW_Q8 + 1, :], hr, hi)
    hr_ref[...] = nhr + fr[SUBLANES - 1:, :]
    hi_ref[...] = nhi + fi[SUBLANES - 1:, :]

    def fix_step(kk, _):
        xr, xi = [], []
        for half in range(2):
            rows = pl.ds(pl.multiple_of((2 * kk + half) * SUBLANES, SUBLANES), SUBLANES)
            pr = par_ref[rows, :]
            pi = pai_ref[rows, :]
            xr.append(st_ref[rows, :N] + (pr * cr - pi * ci))
            xi.append(st_ref[rows, N:] + (pr * ci + pi * cr))
        rows16 = pl.ds(pl.multiple_of(kk * 2 * SUBLANES, 2 * SUBLANES), 2 * SUBLANES)
        xb_ref[rows16, :N] = jnp.concatenate(xr, axis=0).astype(BF16)
        xb_ref[rows16, N:] = jnp.concatenate(xi, axis=0).astype(BF16)
        return 0

    lax.fori_loop(0, SSM_SUB // 2, fix_step, 0)

    y_blocks = []
    for blk in range(SSM_BLOCKS):
        x_blk = jnp.concatenate([xb_ref[:, blk * H:(blk + 1) * H], xb_ref[:, N + blk * H:N + (blk + 1) * H]],
                                axis=-1)
        y_blocks.append(_dot(x_blk, cblk_ref[blk]))
    y_perm = jnp.concatenate(y_blocks, axis=-1)
    y_hi = y_perm.astype(BF16)
    y_lo = (y_perm - y_hi.astype(F32)).astype(BF16)
    y_ref[0] = _dot(permt_ref[...], y_hi) + _dot(permt_ref[...], y_lo) + d_ref[...] * u


def _s5(u, perm, permt, bblk, cblk, pa_r, pa_i, tab_r, tab_i, d_row):
    B, S, W = u.shape
    T = SSM_TILE
    consts = [perm, permt, bblk, cblk, pa_r, pa_i, tab_r, tab_i, d_row]
    return pl.pallas_call(
        _s5_kernel,
        grid=(B, S // T),
        in_specs=[pl.BlockSpec((1, T, W), lambda b, i: (b, i, 0))] + [_const_spec(c.shape) for c in consts],
        out_specs=pl.BlockSpec((1, T, W), lambda b, i: (b, i, 0)),
        out_shape=jax.ShapeDtypeStruct((B, S, W), F32),
        scratch_shapes=[
            pltpu.VMEM((T, 2 * SSM_STATES), F32),
            pltpu.VMEM((T, 2 * SSM_STATES), BF16),
            pltpu.VMEM((1, SSM_STATES), F32),
            pltpu.VMEM((1, SSM_STATES), F32),
        ],
        compiler_params=pltpu.CompilerParams(
            dimension_semantics=("parallel", "arbitrary"), vmem_limit_bytes=VMEM_LIMIT),
        name="s5",
    )(u, *consts)


def _memkv_kernel(mem_ref, gmem_ref, wkv_ref, gk_ref, kt_ref, v_ref):
    m = _rms(mem_ref[0], gmem_ref[...]).astype(BF16)
    kv = _dot(m, wkv_ref[...])
    gk = gk_ref[...]
    for hh in range(XATTN_HEADS):
        kh = kv[:, hh * XATTN_HEAD_DIM:(hh + 1) * XATTN_HEAD_DIM]
        kt_ref[0, hh] = _rms(kh, gk).T.astype(BF16)
    v_ref[0] = kv[:, XATTN_WIDTH:].astype(BF16)


def _memkv(mem, gmem, wkv, gk):
    B, M, D = mem.shape
    return pl.pallas_call(
        _memkv_kernel,
        grid=(B,),
        in_specs=[pl.BlockSpec((1, M, D), lambda b: (b, 0, 0)), _const_spec(gmem.shape),
                  _const_spec(wkv.shape), _const_spec(gk.shape)],
        out_specs=[pl.BlockSpec((1, XATTN_HEADS, XATTN_HEAD_DIM, M), lambda b: (b, 0, 0, 0)),
                   pl.BlockSpec((1, M, XATTN_WIDTH), lambda b: (b, 0, 0))],
        out_shape=[jax.ShapeDtypeStruct((B, XATTN_HEADS, XATTN_HEAD_DIM, M), BF16),
                   jax.ShapeDtypeStruct((B, M, XATTN_WIDTH), BF16)],
        compiler_params=pltpu.CompilerParams(dimension_semantics=("parallel",)),
        name="memkv",
    )(mem, gmem, wkv, gk)


def _back_kernel(x1_ref, omla_ref, yssm_ref, kt_ref, vm_ref, wglu_ref, bglu_ref, gossm_ref, gomla_ref, wo_ref,
                 gx_ref, wq_ref, gqn_ref, wxo_ref, g2_ref, wg_ref, wu_ref, wd_ref, out_ref):
    g = jax.nn.gelu(yssm_ref[0])
    glu = g * jax.nn.sigmoid(_dot(g.astype(BF16), wglu_ref[...]) + bglu_ref[...])
    y_ssm = _rms(glu, gossm_ref[...]).astype(BF16)
    y_mla = _rms(omla_ref[0], gomla_ref[...]).astype(BF16)
    y = jnp.concatenate([y_mla, y_ssm], axis=-1)
    x2 = x1_ref[0] + _dot(y, wo_ref[...])
    hq = _rms(x2, gx_ref[...]).astype(BF16)
    q = _dot(hq, wq_ref[...])
    scale = XATTN_HEAD_DIM ** -0.5
    heads = []
    for hh in range(XATTN_HEADS):
        lo = hh * XATTN_HEAD_DIM
        qh = (_rms(q[:, lo:lo + XATTN_HEAD_DIM], gqn_ref[...]) * scale).astype(BF16)
        s = _dot(qh, kt_ref[0, hh])
        p = jnp.exp(s - jnp.max(s, axis=-1, keepdims=True))
        p = p / jnp.sum(p, axis=-1, keepdims=True)
        heads.append(_dot(p.astype(BF16), vm_ref[0, :, lo:lo + XATTN_HEAD_DIM]))
    o = jnp.concatenate(heads, axis=-1).astype(BF16)
    x3 = x2 + _dot(o, wxo_ref[...])
    out_ref[0] = _ffn_half_step(x3, g2_ref, wg_ref, wu_ref, wd_ref)


def _back(x1, omla, yssm, kt, vm, wglu, bglu, gossm, gomla, wo, gx, wq, gqn, wxo, g2, wg, wu, wd):
    B, S, D = x1.shape
    T = TOKEN_TILE
    M = vm.shape[1]
    row = lambda w: pl.BlockSpec((1, T, w), lambda b, i: (b, i, 0))
    consts = [wglu, bglu, gossm, gomla, wo, gx, wq, gqn, wxo, g2, wg, wu, wd]
    return pl.pallas_call(
        _back_kernel,
        grid=(B, S // T),
        in_specs=[row(D), row(MLA_WIDTH), row(SSM_WIDTH),
                  pl.BlockSpec((1, XATTN_HEADS, XATTN_HEAD_DIM, M), lambda b, i: (b, 0, 0, 0)),
                  pl.BlockSpec((1, M, XATTN_WIDTH), lambda b, i: (b, 0, 0))]
                 + [_const_spec(c.shape) for c in consts],
        out_specs=row(D),
        out_shape=jax.ShapeDtypeStruct((B, S, D), F32),
        compiler_params=pltpu.CompilerParams(
            dimension_semantics=("parallel", "parallel"), vmem_limit_bytes=VMEM_LIMIT),
        name="back",
    )(x1, omla, yssm, kt, vm, *consts)


def _row(v):
    return v.reshape(1, -1).astype(F32)


def _ffn_weights(w_gate, w_up, w_down):
    rows = lambda w: jnp.pad(w.astype(BF16), ((0, D_FF_PAD - D_FF), (0, 0)))
    return rows(w_gate.T), rows(w_up.T), rows(w_down)


def _pad_rope_cols(w):
    half = MLA_ROPE // 2
    z = jnp.zeros(w.shape[:-1] + (half,), w.dtype)
    return jnp.concatenate([w[..., :half], z, w[..., half:], z], axis=-1)


def _pad_head_cols(w):
    return jnp.concatenate([w[..., :MLA_NOPE], _pad_rope_cols(w[..., MLA_NOPE:])], axis=-1)


def _block_diag(blocks):
    G, r, c = blocks.shape
    eye = jnp.eye(G, dtype=blocks.dtype)
    return jnp.einsum('grc,gk->grkc', blocks, eye).reshape(G * r, G * c)


def kernel(x, mem, positions, ffn1_norm, ffn1_w_gate, ffn1_w_up, ffn1_w_down, mix_norm, w_in, mla_q_norm, mla_w_uq, mla_kv_norm, mla_w_ukv, mla_qk_norm_q, mla_qk_norm_k, ssm_a_re, ssm_a_im, ssm_log_dt, ssm_b_re, ssm_b_im, ssm_c_re, ssm_c_im, ssm_d, ssm_w_glu, ssm_b_glu, out_norm_mla, out_norm_ssm, w_o, xattn_norm, mem_norm, xattn_w_q, xattn_w_kv, xattn_q_norm, xattn_k_norm, xattn_w_o, ffn2_norm, ffn2_w_gate, ffn2_w_up, ffn2_w_down):
    depth = ffn1_norm.shape[0]
    B, S, _ = x.shape
    pos3 = positions.reshape(B, S, 1)
    half = MLA_ROPE // 2
    inv = ROPE_THETA ** (-jnp.arange(half, dtype=F32) / half)
    zeros = jnp.zeros((half,), F32)
    ones = jnp.ones((half,), F32)
    inv_tab = jnp.concatenate([inv, zeros, inv, zeros]).reshape(1, ROPE_PAD)
    sgn_tab = jnp.concatenate([-ones, zeros, ones, zeros]).reshape(1, ROPE_PAD)
    j = jnp.arange(SSM_TILE)
    src = (j % SUBLANES) * SSM_SUB + j // SUBLANES
    perm = (src[:, None] == j[None, :]).astype(BF16)
    permt = perm.T

    for l in range(depth):
        f1 = _ffn_weights(ffn1_w_gate[l], ffn1_w_up[l], ffn1_w_down[l])
        f2 = _ffn_weights(ffn2_w_gate[l], ffn2_w_up[l], ffn2_w_down[l])
        o_kr = MLA_Q_RANK + MLA_KV_RANK
        win = jnp.concatenate([w_in[l][:, :o_kr], _pad_rope_cols(w_in[l][:, o_kr:o_kr + MLA_ROPE]),
                               w_in[l][:, o_kr + MLA_ROPE:]], axis=-1).astype(BF16).T
        wuq = _pad_head_cols(mla_w_uq[l].reshape(MLA_Q_RANK, MLA_HEADS, MLA_QK)).reshape(
            MLA_Q_RANK, MLA_HEADS * HEAD_PAD).astype(BF16)
        gqk_q = _row(_pad_head_cols(mla_qk_norm_q[l]))
        gqk_k = _row(_pad_head_cols(mla_qk_norm_k[l]))

        x1, q, k, v, u = _front(
            x, pos3, _row(ffn1_norm[l]), *f1, _row(mix_norm[l]), win, _row(mla_q_norm[l]), wuq,
            _row(mla_kv_norm[l]), mla_w_ukv[l].astype(BF16), gqk_q, gqk_k, inv_tab, sgn_tab)

        o_mla = _attention(q, k, v)

        rows = lambda a: a.reshape(1, SSM_STATES).astype(F32)
        ldt = jnp.repeat(ssm_log_dt[l].astype(F32), SSM_STATE).reshape(1, SSM_STATES)
        bt = lambda b: b.astype(F32).transpose(2, 0, 1).reshape(SSM_GROUP, SSM_STATES)
        pa_r, pa_i, tab_r, tab_i, bb_r, bb_i = _s5prep(
            rows(ssm_a_re[l]), rows(ssm_a_im[l]), ldt, bt(ssm_b_re[l]), bt(ssm_b_im[l]))
        gpb = SSM_GROUPS // SSM_BLOCKS
        b_blocks = lambda m: m.reshape(SSM_GROUP, SSM_GROUPS, SSM_STATE).transpose(1, 0, 2)
        c_blocks = lambda c: c.astype(F32).transpose(0, 2, 1)
        bblk = jnp.stack([
            jnp.concatenate([_block_diag(b_blocks(bb_r)[i * gpb:(i + 1) * gpb]),
                             _block_diag(b_blocks(bb_i)[i * gpb:(i + 1) * gpb])], axis=-1)
            for i in range(SSM_BLOCKS)]).astype(BF16)
        cblk = jnp.stack([
            jnp.concatenate([_block_diag(c_blocks(ssm_c_re[l])[i * gpb:(i + 1) * gpb]),
                             -_block_diag(c_blocks(ssm_c_im[l])[i * gpb:(i + 1) * gpb])], axis=0)
            for i in range(SSM_BLOCKS)]).astype(BF16)
        y_ssm = _s5(u, perm, permt, bblk, cblk, pa_r, pa_i, tab_r, tab_i, _row(ssm_d[l]))

        kt, vm = _memkv(mem, _row(mem_norm[l]), xattn_w_kv[l].astype(BF16), _row(xattn_k_norm[l]))

        x = _back(x1, o_mla, y_ssm, kt, vm, ssm_w_glu[l].astype(BF16), _row(ssm_b_glu[l]), _row(out_norm_ssm[l]),
                  _row(out_norm_mla[l]), w_o[l].astype(BF16), _row(xattn_norm[l]),
                  xattn_w_q[l].astype(BF16), _row(xattn_q_norm[l]), xattn_w_o[l].astype(BF16),
                  _row(ffn2_norm[l]), *f2)
    return x
```

```python
import functools
import math

import jax
import jax.numpy as jnp
from jax import lax
from jax.experimental import pallas as pl
from jax.experimental.pallas import tpu as pltpu

F32 = jnp.float32
BF16 = jnp.bfloat16

D_MODEL = 1024
D_FF = 2752
MLA_HEADS = 4
MLA_Q_RANK = 384
MLA_KV_RANK = 256
MLA_NOPE = 128
MLA_ROPE = 64
MLA_V = 128
MLA_QK = MLA_NOPE + MLA_ROPE
MLA_WIDTH = MLA_HEADS * MLA_V
SSM_WIDTH = D_MODEL - MLA_WIDTH
SSM_GROUP = 16
SSM_GROUPS = SSM_WIDTH // SSM_GROUP
SSM_STATE = 64
XATTN_HEADS = 4
XATTN_HEAD_DIM = 128
XATTN_WIDTH = XATTN_HEADS * XATTN_HEAD_DIM
ROPE_THETA = 10000.0
EPS = 1e-6

LANES = 128
SUBLANES = 8
D_FF_PAD = 2816
FF_CHUNK = D_FF_PAD // 2
HEAD_PAD = 2 * LANES
ROPE_PAD = LANES
IN_PAD = MLA_Q_RANK + MLA_KV_RANK + ROPE_PAD + SSM_WIDTH
TOKEN_TILE = 512
ATTN_TILE = 512
ATTN_HEADS_PER_STEP = 4
LOG2_E = math.log2(math.e)
SSM_STATES = SSM_GROUPS * SSM_STATE
SSM_TILE = 256
SSM_SUB = SSM_TILE // SUBLANES
SSM_SEQS_PER_STEP = 4
SSM_BLOCKS = 2
SSM_HALF = SSM_STATES // SSM_BLOCKS
TABLE_ROWS = 16
ROW_Q8 = SUBLANES
ROW_A32 = ROW_Q8 + 1
ROW_A64 = ROW_Q8 + 2
ROW_A128 = ROW_Q8 + 3
VMEM_LIMIT = 56 * 1024 * 1024


def _rms(x, g):
    return x * lax.rsqrt(jnp.mean(x * x, axis=-1, keepdims=True) + EPS) * g


def _dot(a, b):
    return jnp.dot(a, b, preferred_element_type=F32)


def _dot_nt(a, b_t):
    return lax.dot_general(a, b_t, (((1,), (1,)), ((), ())), preferred_element_type=F32)


def _ffn_half_step(x, g_ref, wgt_ref, wut_ref, wd_ref):
    h = _rms(x, g_ref[...]).astype(BF16)
    acc = None
    for c in range(D_FF_PAD // FF_CHUNK):
        lo = c * FF_CHUNK
        gate = _dot_nt(h, wgt_ref[lo:lo + FF_CHUNK, :])
        up = _dot_nt(h, wut_ref[lo:lo + FF_CHUNK, :])
        act = (gate * jax.nn.sigmoid(gate) * up).astype(BF16)
        part = _dot(act, wd_ref[lo:lo + FF_CHUNK, :])
        acc = part if acc is None else acc + part
    return x + 0.5 * acc


def _rope(x, cos, sin_signed):
    return x * cos + pltpu.roll(x, ROPE_PAD // 2, 1) * sin_signed


def _front_kernel(x_ref, pos_ref, g1_ref, wgt_ref, wut_ref, wd_ref, gmix_ref, wint_ref,
                  gq_ref, wuq_ref, gkv_ref, wukv_ref, gqk_q_ref, gqk_k_ref, inv_ref, sgn_ref,
                  x1_ref, q_ref, k_ref, v_ref, u_ref, qraw_sc, kv_sc, kr_sc):
    @pl.when(pl.program_id(0) == 0)
    def _():
        qraw_sc[...] = jnp.zeros_like(qraw_sc)
        kv_sc[...] = jnp.zeros_like(kv_sc)
        kr_sc[...] = jnp.zeros_like(kr_sc)

    ang = pos_ref[0].astype(F32) * inv_ref[...]
    cos = jnp.cos(ang)
    sin = jnp.sin(ang) * sgn_ref[...]
    gq = gqk_q_ref[...]
    gk = gqk_k_ref[...]
    kr = kr_sc[...]
    kr_ss = jnp.sum(kr * kr, axis=-1, keepdims=True)
    kr_rot = _rope(kr * gk[:, MLA_NOPE:], cos, sin)
    inv_qk = 1.0 / MLA_QK
    scale = MLA_QK ** -0.5 * LOG2_E
    for hh in range(MLA_HEADS):
        qh = qraw_sc[:, hh * HEAD_PAD:(hh + 1) * HEAD_PAD]
        qn = qh * lax.rsqrt(jnp.sum(qh * qh, axis=-1, keepdims=True) * inv_qk + EPS) * gq
        q_rot = _rope(qn[:, MLA_NOPE:], cos, sin)
        q_ref[0, hh] = (jnp.concatenate([qn[:, :MLA_NOPE], q_rot], axis=-1) * scale).astype(BF16)
        kn = kv_sc[:, hh * HEAD_PAD:hh * HEAD_PAD + MLA_NOPE]
        inv_k = lax.rsqrt((jnp.sum(kn * kn, axis=-1, keepdims=True) + kr_ss) * inv_qk + EPS)
        k_ref[0, hh] = jnp.concatenate([kn * inv_k * gk[:, :MLA_NOPE], kr_rot * inv_k], axis=-1).astype(BF16)
        v_ref[0, hh] = kv_sc[:, hh * HEAD_PAD + MLA_NOPE:(hh + 1) * HEAD_PAD].astype(BF16)

    x1 = _ffn_half_step(x_ref[0], g1_ref, wgt_ref, wut_ref, wd_ref)
    x1_ref[0] = x1
    h = _rms(x1, gmix_ref[...]).astype(BF16)
    proj = _dot_nt(h, wint_ref[...])
    o_ckv = MLA_Q_RANK
    o_kr = o_ckv + MLA_KV_RANK
    o_u = o_kr + ROPE_PAD
    u_ref[0] = proj[:, o_u:]
    cq = _rms(proj[:, :o_ckv], gq_ref[...]).astype(BF16)
    ckv = _rms(proj[:, o_ckv:o_kr], gkv_ref[...]).astype(BF16)
    qraw_sc[...] = _dot(cq, wuq_ref[...])
    kv_sc[...] = _dot(ckv, wukv_ref[...])
    kr_sc[...] = proj[:, o_kr:o_u]


def _const_spec(shape):
    return pl.BlockSpec(shape, lambda *_: (0,) * len(shape), pipeline_mode=pl.Buffered(1))


def _front(x, pos3, g1, wgt, wut, wd, gmix, wint, gq, wuq, gkv, wukv, gqk_q, gqk_k, inv_tab, sgn_tab):
    B, S, D = x.shape
    T = TOKEN_TILE
    per_seq = S // T
    n_tiles = B * per_seq

    def cur(s):
        t = jnp.minimum(s, n_tiles - 1)
        return t // per_seq, t % per_seq

    def prev(s):
        t = jnp.maximum(s - 1, 0)
        return t // per_seq, t % per_seq

    row = lambda w, tile: pl.BlockSpec((1, T, w), lambda s: (*tile(s), 0))
    head = lambda w: pl.BlockSpec((1, MLA_HEADS, T, w), lambda s: (prev(s)[0], 0, prev(s)[1], 0))
    consts = [g1, wgt, wut, wd, gmix, wint, gq, wuq, gkv, wukv, gqk_q, gqk_k, inv_tab, sgn_tab]
    return pl.pallas_call(
        _front_kernel,
        grid=(n_tiles + 1,),
        in_specs=[row(D, cur), row(1, prev)] + [_const_spec(c.shape) for c in consts],
        out_specs=[row(D, cur), head(HEAD_PAD), head(HEAD_PAD), head(MLA_V), row(SSM_WIDTH, cur)],
        out_shape=[
            jax.ShapeDtypeStruct((B, S, D), F32),
            jax.ShapeDtypeStruct((B, MLA_HEADS, S, HEAD_PAD), BF16),
            jax.ShapeDtypeStruct((B, MLA_HEADS, S, HEAD_PAD), BF16),
            jax.ShapeDtypeStruct((B, MLA_HEADS, S, MLA_V), BF16),
            jax.ShapeDtypeStruct((B, S, SSM_WIDTH), F32),
        ],
        scratch_shapes=[pltpu.VMEM((T, MLA_HEADS * HEAD_PAD), F32), pltpu.VMEM((T, MLA_HEADS * HEAD_PAD), F32),
                        pltpu.VMEM((T, ROPE_PAD), F32)],
        compiler_params=pltpu.CompilerParams(dimension_semantics=("arbitrary",), vmem_limit_bytes=VMEM_LIMIT),
        name="front",
    )(x, pos3, *consts)


def _attn_kernel(q_ref, k_ref, v_ref, o_ref, m_ref, acc_ref, alpha_a, p_a, alpha_b, p_b):
    T = ATTN_TILE
    G = ATTN_HEADS_PER_STEP
    qi = pl.program_id(2)
    m_ref[...] = jnp.full(m_ref.shape, -jnp.inf, F32)
    acc_ref[...] = jnp.zeros(acc_ref.shape, F32)
    ones = jnp.ones((T, LANES), BF16)

    def tile_rows(j):
        return pl.ds(pl.multiple_of(j * T, T), T)

    def scores(j):
        return [lax.dot_general(q_ref[0, g], k_ref[0, g, tile_rows(j), :], (((1,), (1,)), ((), ())),
                                preferred_element_type=F32) for g in range(G)]

    def softmax(s_all, alpha_ref, p_ref, masked):
        for g, s in enumerate(s_all):
            if masked:
                rows = lax.broadcasted_iota(jnp.int32, (T, T), 0)
                cols = lax.broadcasted_iota(jnp.int32, (T, T), 1)
                s = jnp.where(rows >= cols, s, -jnp.inf)
            m_prev = m_ref[g]
            m_new = jnp.maximum(m_prev, jnp.max(s, axis=-1, keepdims=True))
            alpha_ref[g] = jnp.exp2(m_prev - m_new)
            p_ref[g] = jnp.exp2(s - jnp.tile(m_new, (1, T // LANES))).astype(BF16)
            m_ref[g] = m_new

    def values(j, alpha_ref, p_ref):
        for g in range(G):
            v_ext = jnp.concatenate([v_ref[0, g, tile_rows(j), :], ones], axis=-1)
            acc_ref[g] = jnp.tile(alpha_ref[g], (1, 2)) * acc_ref[g] + _dot(p_ref[g], v_ext)

    def advance(j, prev, prev_bufs, next_bufs):
        s_all = scores(j)
        values(prev, *prev_bufs)
        softmax(s_all, *next_bufs, masked=False)

    buf_a = (alpha_a, p_a)
    buf_b = (alpha_b, p_b)
    softmax(scores(qi), *buf_a, masked=True)

    def pair(i, prev):
        advance(2 * i, prev, buf_a, buf_b)
        advance(2 * i + 1, 2 * i, buf_b, buf_a)
        return 2 * i + 1

    prev = lax.fori_loop(0, qi // 2, pair, qi)

    @pl.when(qi % 2 == 1)
    def _():
        advance(qi - 1, prev, buf_a, buf_b)
        values(qi - 1, *buf_b)

    @pl.when(qi % 2 == 0)
    def _():
        values(prev, *buf_a)

    for g in range(G):
        acc = acc_ref[g]
        o_ref[0, :, g * MLA_V:(g + 1) * MLA_V] = acc[:, :MLA_V] / acc[:, MLA_V:]


def _attention(q, k, v):
    B, H, S, _ = q.shape
    T = ATTN_TILE
    G = ATTN_HEADS_PER_STEP
    return pl.pallas_call(
        _attn_kernel,
        grid=(B, H // G, S // T),
        in_specs=[
            pl.BlockSpec((1, G, T, HEAD_PAD), lambda b, h, i: (b, h, i, 0)),
            pl.BlockSpec((1, G, S, HEAD_PAD), lambda b, h, i: (b, h, 0, 0)),
            pl.BlockSpec((1, G, S, MLA_V), lambda b, h, i: (b, h, 0, 0)),
        ],
        out_specs=pl.BlockSpec((1, T, G * MLA_V), lambda b, h, i: (b, i, h)),
        out_shape=jax.ShapeDtypeStruct((B, S, H * MLA_V), F32),
        scratch_shapes=[pltpu.VMEM((G, T, LANES), F32), pltpu.VMEM((G, T, 2 * MLA_V), F32),
                        pltpu.VMEM((G, T, LANES), F32), pltpu.VMEM((G, T, T), BF16),
                        pltpu.VMEM((G, T, LANES), F32), pltpu.VMEM((G, T, T), BF16)],
        compiler_params=pltpu.CompilerParams(
            dimension_semantics=("parallel", "parallel", "parallel"), vmem_limit_bytes=VMEM_LIMIT),
        name="attn",
    )(q, k, v)


def _cmul(ar, ai, br, bi):
    return ar * br - ai * bi, ar * bi + ai * br


def _s5prep_kernel(lr_ref, li_ref, ldt_ref, btr_ref, bti_ref, par_ref, pai_ref, tr_ref, ti_ref, bbr_ref, bbi_ref):
    lr = lr_ref[...]
    li = li_ref[...]
    dt = jnp.exp(ldt_ref[...])
    decay = jnp.exp(lr * dt)
    ar = decay * jnp.cos(li * dt)
    ai = decay * jnp.sin(li * dt)
    den = lr * lr + li * li
    nr = ar - 1.0
    coef_r = (nr * lr + ai * li) / den
    coef_i = (ai * lr - nr * li) / den
    br = btr_ref[...]
    bi = bti_ref[...]
    bbr_ref[...] = coef_r * br - coef_i * bi
    bbi_ref[...] = coef_r * bi + coef_i * br
    pr, pi = ar, ai
    for k in range(SSM_SUB):
        if k:
            pr, pi = _cmul(pr, pi, ar, ai)
        par_ref[k * SUBLANES:(k + 1) * SUBLANES, :] = jnp.broadcast_to(pr, (SUBLANES, SSM_STATES))
        pai_ref[k * SUBLANES:(k + 1) * SUBLANES, :] = jnp.broadcast_to(pi, (SUBLANES, SSM_STATES))
    a32r, a32i = pr, pi
    qr, qi = jnp.ones_like(ar), jnp.zeros_like(ar)
    for r in range(SUBLANES + 1):
        tr_ref[r:r + 1, :] = qr
        ti_ref[r:r + 1, :] = qi
        qr, qi = _cmul(qr, qi, a32r, a32i)
    a64r, a64i = _cmul(a32r, a32i, a32r, a32i)
    a128r, a128i = _cmul(a64r, a64i, a64r, a64i)
    for row, (vr, vi) in ((ROW_A32, (a32r, a32i)), (ROW_A64, (a64r, a64i)), (ROW_A128, (a128r, a128i))):
        tr_ref[row:row + 1, :] = vr
        ti_ref[row:row + 1, :] = vi
    zeros = jnp.zeros((TABLE_ROWS - ROW_A128 - 1, SSM_STATES), F32)
    tr_ref[ROW_A128 + 1:, :] = zeros
    ti_ref[ROW_A128 + 1:, :] = zeros


def _s5prep(lr, li, ldt, btr, bti):
    pa = jax.ShapeDtypeStruct((SSM_TILE, SSM_STATES), F32)
    tab = jax.ShapeDtypeStruct((TABLE_ROWS, SSM_STATES), F32)
    bb = jax.ShapeDtypeStruct((SSM_GROUP, SSM_STATES), F32)
    return pl.pallas_call(_s5prep_kernel, out_shape=[pa, pa, tab, tab, bb, bb], name="s5prep")(
        lr, li, ldt, btr, bti)


def _s5_kernel(u_ref, perm_ref, permt_ref, bblk_ref, cblk_ref, par_ref, pai_ref, tr_ref, ti_ref, d_ref,
               y_ref, st_all, xb_all, hr_all, hi_all):
    @pl.when(pl.program_id(1) == 0)
    def _():
        hr_all[...] = jnp.zeros_like(hr_all)
        hi_all[...] = jnp.zeros_like(hi_all)

    for q in range(SSM_SEQS_PER_STEP):
        _s5_tile(u_ref.at[q], perm_ref, permt_ref, bblk_ref, cblk_ref, par_ref, pai_ref, tr_ref, ti_ref, d_ref,
                 y_ref.at[q], st_all.at[q], xb_all.at[q], hr_all.at[q], hi_all.at[q])


def _s5_tile(u_ref, perm_ref, permt_ref, bblk_ref, cblk_ref, par_ref, pai_ref, tr_ref, ti_ref, d_ref,
             y_ref, st_ref, xb_ref, hr_ref, hi_ref):
    N = SSM_STATES
    H = SSM_HALF
    W = SSM_WIDTH // SSM_BLOCKS
    u = u_ref[...]
    u_perm = _dot(perm_ref[...], u.astype(BF16)).astype(BF16)
    for blk in range(SSM_BLOCKS):
        bu = _dot(u_perm[:, blk * W:(blk + 1) * W], bblk_ref[blk])
        st_ref[:, blk * H:(blk + 1) * H] = bu[:, :H]
        st_ref[:, N + blk * H:N + (blk + 1) * H] = bu[:, H:]

    fr = fi = None
    for k in range(SSM_SUB):
        rows = slice(k * SUBLANES, (k + 1) * SUBLANES)
        if k:
            ar = par_ref[0:SUBLANES, :]
            ai = pai_ref[0:SUBLANES, :]
            fr, fi = (ar * fr - ai * fi + st_ref[rows, :N], ar * fi + ai * fr + st_ref[rows, N:])
            st_ref[rows, :N] = fr
            st_ref[rows, N:] = fi
        else:
            fr, fi = st_ref[rows, :N], st_ref[rows, N:]

    ridx = lax.broadcasted_iota(jnp.int32, (SUBLANES, N), 0)

    def shifted(x, s):
        return jnp.where(ridx >= s, pltpu.roll(x, s, 0), 0.0)

    for s, row in ((1, ROW_A32), (2, ROW_A64), (4, ROW_A128)):
        mr, mi = _cmul(tr_ref[row:row + 1, :], ti_ref[row:row + 1, :], shifted(fr, s), shifted(fi, s))
        fr, fi = fr + mr, fi + mi
    hr = hr_ref[...]
    hi = hi_ref[...]
    cr, ci = _cmul(tr_ref[0:ROW_Q8, :], ti_ref[0:ROW_Q8, :], hr, hi)
    cr = cr + shifted(fr, 1)
    ci = ci + shifted(fi, 1)
    nhr, nhi = _cmul(tr_ref[ROW_Q8:ROW_Q8 + 1, :], ti_ref[ROW_Q8:ROW_Q8 + 1, :], hr, hi)
    hr_ref[...] = nhr + fr[SUBLANES - 1:, :]
    hi_ref[...] = nhi + fi[SUBLANES - 1:, :]

    for kk in range(SSM_SUB // 2):
        xr, xi = [], []
        for half in range(2):
            rows = slice((2 * kk + half) * SUBLANES, (2 * kk + half + 1) * SUBLANES)
            pr = par_ref[rows, :]
            pi = pai_ref[rows, :]
            xr.append(st_ref[rows, :N] + (pr * cr - pi * ci))
            xi.append(st_ref[rows, N:] + (pr * ci + pi * cr))
        rows16 = slice(kk * 2 * SUBLANES, (kk + 1) * 2 * SUBLANES)
        xb_ref[rows16, :N] = jnp.concatenate(xr, axis=0).astype(BF16)
        xb_ref[rows16, N:] = jnp.concatenate(xi, axis=0).astype(BF16)

    y_blocks = []
    for blk in range(SSM_BLOCKS):
        x_blk = jnp.concatenate([xb_ref[:, blk * H:(blk + 1) * H], xb_ref[:, N + blk * H:N + (blk + 1) * H]],
                                axis=-1)
        y_blocks.append(_dot(x_blk, cblk_ref[blk]))
    y_perm = jnp.concatenate(y_blocks, axis=-1)
    y_hi = y_perm.astype(BF16)
    y_lo = (y_perm - y_hi.astype(F32)).astype(BF16)
    y_ref[...] = _dot(permt_ref[...], y_hi) + _dot(permt_ref[...], y_lo) + d_ref[...] * u


def _s5(u, perm, permt, bblk, cblk, pa_r, pa_i, tab_r, tab_i, d_row):
    B, S, W = u.shape
    T = SSM_TILE
    Q = SSM_SEQS_PER_STEP
    consts = [perm, permt, bblk, cblk, pa_r, pa_i, tab_r, tab_i, d_row]
    return pl.pallas_call(
        _s5_kernel,
        grid=(B // Q, S // T),
        in_specs=[pl.BlockSpec((Q, T, W), lambda b, i: (b, i, 0))] + [_const_spec(c.shape) for c in consts],
        out_specs=pl.BlockSpec((Q, T, W), lambda b, i: (b, i, 0)),
        out_shape=jax.ShapeDtypeStruct((B, S, W), F32),
        scratch_shapes=[
            pltpu.VMEM((Q, T, 2 * SSM_STATES), F32),
            pltpu.VMEM((Q, T, 2 * SSM_STATES), BF16),
            pltpu.VMEM((Q, 1, SSM_STATES), F32),
            pltpu.VMEM((Q, 1, SSM_STATES), F32),
        ],
        compiler_params=pltpu.CompilerParams(
            dimension_semantics=("parallel", "arbitrary"), vmem_limit_bytes=VMEM_LIMIT),
        name="s5",
    )(u, *consts)


def _memkv_kernel(mem_ref, gmem_ref, wkv_ref, gk_ref, kt_ref, v_ref):
    m = _rms(mem_ref[0], gmem_ref[...]).astype(BF16)
    kv = _dot(m, wkv_ref[...])
    gk = gk_ref[...]
    for hh in range(XATTN_HEADS):
        kh = kv[:, hh * XATTN_HEAD_DIM:(hh + 1) * XATTN_HEAD_DIM]
        kt_ref[0, hh] = _rms(kh, gk).T.astype(BF16)
    v_ref[0] = kv[:, XATTN_WIDTH:].astype(BF16)


def _memkv(mem, gmem, wkv, gk):
    B, M, D = mem.shape
    return pl.pallas_call(
        _memkv_kernel,
        grid=(B,),
        in_specs=[pl.BlockSpec((1, M, D), lambda b: (b, 0, 0)), _const_spec(gmem.shape),
                  _const_spec(wkv.shape), _const_spec(gk.shape)],
        out_specs=[pl.BlockSpec((1, XATTN_HEADS, XATTN_HEAD_DIM, M), lambda b: (b, 0, 0, 0)),
                   pl.BlockSpec((1, M, XATTN_WIDTH), lambda b: (b, 0, 0))],
        out_shape=[jax.ShapeDtypeStruct((B, XATTN_HEADS, XATTN_HEAD_DIM, M), BF16),
                   jax.ShapeDtypeStruct((B, M, XATTN_WIDTH), BF16)],
        compiler_params=pltpu.CompilerParams(dimension_semantics=("parallel",)),
        name="memkv",
    )(mem, gmem, wkv, gk)


def _back_kernel(x1_ref, omla_ref, yssm_ref, kt_ref, vm_ref, wglu_ref, bglu_ref, gossm_ref, gomla_ref, wo_ref,
                 gx_ref, wq_ref, gqn_ref, wxo_ref, g2_ref, wg_ref, wu_ref, wd_ref, out_ref):
    scale = XATTN_HEAD_DIM ** -0.5

    def mix_and_cross_attend(rows):
        g = jax.nn.gelu(yssm_ref[0, rows, :])
        glu = g * jax.nn.sigmoid(_dot(g.astype(BF16), wglu_ref[...]) + bglu_ref[...])
        y_ssm = _rms(glu, gossm_ref[...]).astype(BF16)
        y_mla = _rms(omla_ref[0, rows, :], gomla_ref[...]).astype(BF16)
        y = jnp.concatenate([y_mla, y_ssm], axis=-1)
        x2 = x1_ref[0, rows, :] + _dot(y, wo_ref[...])
        hq = _rms(x2, gx_ref[...]).astype(BF16)
        q = _dot(hq, wq_ref[...])
        heads = []
        for hh in range(XATTN_HEADS):
            lo = hh * XATTN_HEAD_DIM
            qh = (_rms(q[:, lo:lo + XATTN_HEAD_DIM], gqn_ref[...]) * scale).astype(BF16)
            s = _dot(qh, kt_ref[0, hh])
            p = jnp.exp(s - jnp.max(s, axis=-1, keepdims=True))
            p = p / jnp.sum(p, axis=-1, keepdims=True)
            heads.append(_dot(p.astype(BF16), vm_ref[0, :, lo:lo + XATTN_HEAD_DIM]))
        o = jnp.concatenate(heads, axis=-1).astype(BF16)
        return x2 + _dot(o, wxo_ref[...])

    x3 = mix_and_cross_attend(pl.ds(0, TOKEN_TILE))
    out_ref[0] = _ffn_half_step(x3, g2_ref, wg_ref, wu_ref, wd_ref)


def _back(x1, omla, yssm, kt, vm, wglu, bglu, gossm, gomla, wo, gx, wq, gqn, wxo, g2, wg, wu, wd):
    B, S, D = x1.shape
    T = TOKEN_TILE
    M = vm.shape[1]
    row = lambda w: pl.BlockSpec((1, T, w), lambda b, i: (b, i, 0))
    consts = [wglu, bglu, gossm, gomla, wo, gx, wq, gqn, wxo, g2, wg, wu, wd]
    return pl.pallas_call(
        _back_kernel,
        grid=(B, S // T),
        in_specs=[row(D), row(MLA_WIDTH), row(SSM_WIDTH),
                  pl.BlockSpec((1, XATTN_HEADS, XATTN_HEAD_DIM, M), lambda b, i: (b, 0, 0, 0)),
                  pl.BlockSpec((1, M, XATTN_WIDTH), lambda b, i: (b, 0, 0))]
                 + [_const_spec(c.shape) for c in consts],
        out_specs=row(D),
        out_shape=jax.ShapeDtypeStruct((B, S, D), F32),
        compiler_params=pltpu.CompilerParams(
            dimension_semantics=("parallel", "parallel"), vmem_limit_bytes=VMEM_LIMIT),
        name="back",
    )(x1, omla, yssm, kt, vm, *consts)


def _row(v):
    return v.reshape(1, -1).astype(F32)


def _ffn_weights(w_gate, w_up, w_down):
    rows = lambda w: jnp.pad(w.astype(BF16), ((0, D_FF_PAD - D_FF), (0, 0)))
    return rows(w_gate.T), rows(w_up.T), rows(w_down)


def _pad_rope_cols(w):
    half = MLA_ROPE // 2
    z = jnp.zeros(w.shape[:-1] + (half,), w.dtype)
    return jnp.concatenate([w[..., :half], z, w[..., half:], z], axis=-1)


def _pad_head_cols(w):
    return jnp.concatenate([w[..., :MLA_NOPE], _pad_rope_cols(w[..., MLA_NOPE:])], axis=-1)


def _block_diag(blocks):
    G, r, c = blocks.shape
    eye = jnp.eye(G, dtype=blocks.dtype)
    return jnp.einsum('grc,gk->grkc', blocks, eye).reshape(G * r, G * c)


def kernel(x, mem, positions, ffn1_norm, ffn1_w_gate, ffn1_w_up, ffn1_w_down, mix_norm, w_in, mla_q_norm, mla_w_uq, mla_kv_norm, mla_w_ukv, mla_qk_norm_q, mla_qk_norm_k, ssm_a_re, ssm_a_im, ssm_log_dt, ssm_b_re, ssm_b_im, ssm_c_re, ssm_c_im, ssm_d, ssm_w_glu, ssm_b_glu, out_norm_mla, out_norm_ssm, w_o, xattn_norm, mem_norm, xattn_w_q, xattn_w_kv, xattn_q_norm, xattn_k_norm, xattn_w_o, ffn2_norm, ffn2_w_gate, ffn2_w_up, ffn2_w_down):
    depth = ffn1_norm.shape[0]
    B, S, _ = x.shape
    pos3 = positions.reshape(B, S, 1)
    half = MLA_ROPE // 2
    inv = ROPE_THETA ** (-jnp.arange(half, dtype=F32) / half)
    zeros = jnp.zeros((half,), F32)
    ones = jnp.ones((half,), F32)
    inv_tab = jnp.concatenate([inv, zeros, inv, zeros]).reshape(1, ROPE_PAD)
    sgn_tab = jnp.concatenate([-ones, zeros, ones, zeros]).reshape(1, ROPE_PAD)
    j = jnp.arange(SSM_TILE)
    src = (j % SUBLANES) * SSM_SUB + j // SUBLANES
    perm = (src[:, None] == j[None, :]).astype(BF16)
    permt = perm.T

    for l in range(depth):
        f1 = _ffn_weights(ffn1_w_gate[l], ffn1_w_up[l], ffn1_w_down[l])
        f2 = _ffn_weights(ffn2_w_gate[l], ffn2_w_up[l], ffn2_w_down[l])
        o_kr = MLA_Q_RANK + MLA_KV_RANK
        win = jnp.concatenate([w_in[l][:, :o_kr], _pad_rope_cols(w_in[l][:, o_kr:o_kr + MLA_ROPE]),
                               w_in[l][:, o_kr + MLA_ROPE:]], axis=-1).astype(BF16).T
        wuq = _pad_head_cols(mla_w_uq[l].reshape(MLA_Q_RANK, MLA_HEADS, MLA_QK)).reshape(
            MLA_Q_RANK, MLA_HEADS * HEAD_PAD).astype(BF16)
        gqk_q = _row(_pad_head_cols(mla_qk_norm_q[l]))
        gqk_k = _row(_pad_head_cols(mla_qk_norm_k[l]))

        x1, q, k, v, u = _front(
            x, pos3, _row(ffn1_norm[l]), *f1, _row(mix_norm[l]), win, _row(mla_q_norm[l]), wuq,
            _row(mla_kv_norm[l]), mla_w_ukv[l].astype(BF16), gqk_q, gqk_k, inv_tab, sgn_tab)

        o_mla = _attention(q, k, v)

        rows = lambda a: a.reshape(1, SSM_STATES).astype(F32)
        ldt = jnp.repeat(ssm_log_dt[l].astype(F32), SSM_STATE).reshape(1, SSM_STATES)
        bt = lambda b: b.astype(F32).transpose(2, 0, 1).reshape(SSM_GROUP, SSM_STATES)
        pa_r, pa_i, tab_r, tab_i, bb_r, bb_i = _s5prep(
            rows(ssm_a_re[l]), rows(ssm_a_im[l]), ldt, bt(ssm_b_re[l]), bt(ssm_b_im[l]))
        gpb = SSM_GROUPS // SSM_BLOCKS
        b_blocks = lambda m: m.reshape(SSM_GROUP, SSM_GROUPS, SSM_STATE).transpose(1, 0, 2)
        c_blocks = lambda c: c.astype(F32).transpose(0, 2, 1)
        bblk = jnp.stack([
            jnp.concatenate([_block_diag(b_blocks(bb_r)[i * gpb:(i + 1) * gpb]),
                             _block_diag(b_blocks(bb_i)[i * gpb:(i + 1) * gpb])], axis=-1)
            for i in range(SSM_BLOCKS)]).astype(BF16)
        cblk = jnp.stack([
            jnp.concatenate([_block_diag(c_blocks(ssm_c_re[l])[i * gpb:(i + 1) * gpb]),
                             -_block_diag(c_blocks(ssm_c_im[l])[i * gpb:(i + 1) * gpb])], axis=0)
            for i in range(SSM_BLOCKS)]).astype(BF16)
        y_ssm = _s5(u, perm, permt, bblk, cblk, pa_r, pa_i, tab_r, tab_i, _row(ssm_d[l]))

        kt, vm = _memkv(mem, _row(mem_norm[l]), xattn_w_kv[l].astype(BF16), _row(xattn_k_norm[l]))

        x = _back(x1, o_mla, y_ssm, kt, vm, ssm_w_glu[l].astype(BF16), _row(ssm_b_glu[l]), _row(out_norm_ssm[l]),
                  _row(out_norm_mla[l]), w_o[l].astype(BF16), _row(xattn_norm[l]),
                  xattn_w_q[l].astype(BF16), _row(xattn_q_norm[l]), xattn_w_o[l].astype(BF16),
                  _row(ffn2_norm[l]), *f2)
    return x
```
